```python
import jax, jax.numpy as jnp
from jax import lax
import numpy as np

D_MODEL = 1024
BATCH = 8
SEQ = 2048
DEPTH = 2

D_MIX = D_MODEL
ATTN_DIM = D_MIX // 2
CONV_DIM = D_MIX - ATTN_DIM
HEAD_DIM = 64
N_HEADS = ATTN_DIM // HEAD_DIM
CONV_WIDTH = 31
Q_BLOCK = 128
PLE_DIM = 256
D_IN = 4 * ATTN_DIM + 3 * CONV_DIM
EPS = 1e-6

kernel_name = "hymba_conformer_stickbreaking_ple"


def rms_norm(x, g):
    xf = x.astype(jnp.float32)
    y = xf * lax.rsqrt(jnp.mean(xf * xf, axis=-1, keepdims=True) + EPS)
    return (y * g.astype(jnp.float32)).astype(x.dtype)


def layer_norm(x, g, b):
    xf = x.astype(jnp.float32)
    mu = jnp.mean(xf, axis=-1, keepdims=True)
    xc = xf - mu
    y = xc * lax.rsqrt(jnp.mean(xc * xc, axis=-1, keepdims=True) + EPS)
    return (y * g.astype(jnp.float32) + b.astype(jnp.float32)).astype(x.dtype)


def stick_breaking_attention(q, k, v):
    S = q.shape[1]
    scale = HEAD_DIM ** -0.5
    outs = []
    for blk in range(S // Q_BLOCK):
        q0 = blk * Q_BLOCK
        kend = q0 + Q_BLOCK
        qb = q[:, q0:kend]
        kb = k[:, :kend]
        vb = v[:, :kend]
        z = jnp.einsum('bqhd,bkhd->bhqk', qb, kb).astype(jnp.float32) * scale
        qpos = q0 + jnp.arange(Q_BLOCK)[:, None]
        kpos = jnp.arange(kend)[None, :]
        causal = kpos < qpos
        log_1m_beta = jnp.where(causal, -jax.nn.softplus(z), 0.0)
        suffix = lax.cumsum(log_1m_beta, axis=3, reverse=True) - log_1m_beta
        log_a = jax.nn.log_sigmoid(z) + suffix
        a = jnp.where(causal, jnp.exp(log_a), 0.0)
        outs.append(jnp.einsum('bhqk,bkhd->bqhd', a.astype(v.dtype), vb))
    return jnp.concatenate(outs, axis=1)


def causal_depthwise_conv(x, w, b):
    rhs = w[:, None, :].astype(x.dtype)
    y = lax.conv_general_dilated(
        x, rhs, window_strides=(1,), padding=((CONV_WIDTH - 1, 0),),
        dimension_numbers=('NWC', 'WIO', 'NWC'), feature_group_count=x.shape[-1])
    return y + b.astype(x.dtype)


def setup_inputs(seed: int = 0) -> dict:
    key = jax.random.key(seed)
    ks = jax.random.split(key, 16)
    f32 = jnp.float32
    nrm = lambda k, shape, s: jax.random.normal(k, shape, f32) * s
    return {
        "x": nrm(ks[0], (BATCH, SEQ, D_MODEL), 1.0),
        "p": nrm(ks[1], (DEPTH, BATCH, SEQ, PLE_DIM), 1.0),
        "norm_g": 1.0 + nrm(ks[2], (DEPTH, D_MODEL), 0.02),
        "w_in": nrm(ks[3], (DEPTH, D_MODEL, D_IN), D_MODEL ** -0.5),
        "attn_out_g": 1.0 + nrm(ks[4], (DEPTH, HEAD_DIM), 0.02),
        "dw_w": nrm(ks[5], (DEPTH, CONV_WIDTH, CONV_DIM), CONV_WIDTH ** -0.5),
        "dw_b": nrm(ks[6], (DEPTH, CONV_DIM), 0.02),
        "conv_ln_g": 1.0 + nrm(ks[7], (DEPTH, CONV_DIM), 0.02),
        "conv_ln_b": nrm(ks[8], (DEPTH, CONV_DIM), 0.02),
        "w_pw": nrm(ks[9], (DEPTH, CONV_DIM, CONV_DIM), CONV_DIM ** -0.5),
        "conv_out_g": 1.0 + nrm(ks[10], (DEPTH, CONV_DIM), 0.02),
        "w_out": nrm(ks[11], (DEPTH, D_MIX, D_MODEL), D_MIX ** -0.5),
        "ple_norm_g": 1.0 + nrm(ks[12], (DEPTH, D_MODEL), 0.02),
        "w_ple_gate": nrm(ks[13], (DEPTH, D_MODEL, D_MODEL), D_MODEL ** -0.5),
        "w_ple": nrm(ks[14], (DEPTH, PLE_DIM, D_MODEL), PLE_DIM ** -0.5),
        "final_g": 1.0 + nrm(ks[15], (D_MODEL,), 0.02),
    }


def reference(x, p, norm_g, w_in, attn_out_g, dw_w, dw_b, conv_ln_g, conv_ln_b,
              w_pw, conv_out_g, w_out, ple_norm_g, w_ple_gate, w_ple, final_g):
    B, S, _ = x.shape
    split_at = np.cumsum([ATTN_DIM, ATTN_DIM, ATTN_DIM, ATTN_DIM,
                          CONV_DIM, CONV_DIM])
    h = x
    for i in range(DEPTH):
        hn = rms_norm(h, norm_g[i])
        u = hn @ w_in[i]
        q, k, v, g_attn, c_val, c_gate, g_conv = jnp.split(u, split_at, axis=-1)

        heads = lambda t: t.reshape(B, S, N_HEADS, HEAD_DIM)
        o = stick_breaking_attention(heads(q), heads(k), heads(v))
        o = rms_norm(o, attn_out_g[i]).reshape(B, S, ATTN_DIM)
        y_attn = o * jax.nn.silu(g_attn)

        c = c_val * jax.nn.sigmoid(c_gate)
        c = causal_depthwise_conv(c, dw_w[i], dw_b[i])
        c = jax.nn.silu(layer_norm(c, conv_ln_g[i], conv_ln_b[i]))
        c = c @ w_pw[i]
        y_conv = rms_norm(c, conv_out_g[i]) * jax.nn.silu(g_conv)

        y = jnp.concatenate([y_attn, y_conv], axis=-1) @ w_out[i]
        h = h + y

        gate = jax.nn.sigmoid(rms_norm(h, ple_norm_g[i]) @ w_ple_gate[i])
        h = h + (p[i].astype(h.dtype) @ w_ple[i]) * gate
    return rms_norm(h, final_g)
```

```python
import functools
import math

import jax
import jax.numpy as jnp
from jax import lax
from jax.experimental import pallas as pl
from jax.experimental.pallas import tpu as pltpu

D_MODEL = 1024
ATTN_DIM = 512
CONV_DIM = 512
HEAD_DIM = 64
CONV_WIDTH = 31
PLE_DIM = 256
D_IN = 4 * ATTN_DIM + 3 * CONV_DIM
EPS = 1e-6

LANES = 128
Q_BLOCK = 128
K_CHUNK = 128
HALO = 32
ROW_TILE = 512
CONV_TILE = 256
VMEM_LIMIT = 56 * 1024 * 1024

F32 = jnp.float32
BF16 = jnp.bfloat16


def _sigmoid(x):
    return 1.0 / (1.0 + jnp.exp(-x))


def _in_proj_kernel(x_ref, g_ref, w_ref, q_ref, k_ref, v_ref, sga_ref, c_ref, sgc_ref):
    x = x_ref[...]
    ms = jnp.mean(x * x, axis=-1, keepdims=True)
    hn = (x * lax.rsqrt(ms + EPS) * g_ref[...]).astype(BF16)

    def seg(i):
        return jnp.dot(hn, w_ref[:, i * ATTN_DIM:(i + 1) * ATTN_DIM], preferred_element_type=F32)

    q_ref[...] = (seg(0) * (HEAD_DIM ** -0.5)).astype(BF16)
    k_ref[...] = seg(1).astype(BF16)
    v_ref[...] = seg(2).astype(BF16)
    ga = seg(3)
    sga_ref[...] = ga * _sigmoid(ga)
    c_ref[...] = seg(4) * _sigmoid(seg(5))
    gc = seg(6)
    sgc_ref[...] = gc * _sigmoid(gc)


def _in_proj(h2d, g, w_bf16):
    m = h2d.shape[0]
    row = lambda i: (i, 0)
    const = lambda i: (0, 0)
    seg_spec = pl.BlockSpec((ROW_TILE, ATTN_DIM), row)
    return pl.pallas_call(
        _in_proj_kernel,
        grid=(m // ROW_TILE,),
        in_specs=[
            pl.BlockSpec((ROW_TILE, D_MODEL), row),
            pl.BlockSpec((1, D_MODEL), const),
            pl.BlockSpec((D_MODEL, D_IN), const),
        ],
        out_specs=[seg_spec] * 6,
        out_shape=[
            jax.ShapeDtypeStruct((m, ATTN_DIM), BF16),
            jax.ShapeDtypeStruct((m, ATTN_DIM), BF16),
            jax.ShapeDtypeStruct((m, ATTN_DIM), BF16),
            jax.ShapeDtypeStruct((m, ATTN_DIM), F32),
            jax.ShapeDtypeStruct((m, CONV_DIM), F32),
            jax.ShapeDtypeStruct((m, CONV_DIM), F32),
        ],
        compiler_params=pltpu.CompilerParams(
            dimension_semantics=("arbitrary",), vmem_limit_bytes=VMEM_LIMIT),
        name="in_proj",
    )(h2d, g, w_bf16)


def _attn_kernel(q_ref, k_ref, v_ref, sg_ref, g_ref, hm_ref, w_ref, o_ref, acc_ref, carry_ref):
    n_blocks = q_ref.shape[0] // Q_BLOCK
    row = lax.broadcasted_iota(jnp.int32, (Q_BLOCK, K_CHUNK), 0)
    col = lax.broadcasted_iota(jnp.int32, (Q_BLOCK, K_CHUNK), 1)
    causal = col < row
    lane = lax.broadcasted_iota(jnp.int32, (Q_BLOCK, LANES), 1)
    first_head = lane < HEAD_DIM

    def chunk(qh, j, diag):
        k0 = pl.multiple_of(j * K_CHUNK, K_CHUNK)
        kc = k_ref[pl.ds(k0, K_CHUNK), :]
        vc = v_ref[pl.ds(k0, K_CHUNK), :]
        for h in range(2):
            z = lax.dot_general(qh[h], kc, (((1,), (1,)), ((), ())), preferred_element_type=F32)
            sp = jnp.maximum(z, 0.0) + jnp.log(1.0 + jnp.exp(-jnp.abs(z)))
            if diag:
                sp = jnp.where(causal, sp, 0.0)
            hi = sp.astype(BF16)
            lo = (sp - hi.astype(F32)).astype(BF16)
            r = jnp.dot(jnp.concatenate([hi, lo], axis=1), w_ref[...], preferred_element_type=F32)
            c = carry_ref[h]
            p = jnp.exp(z + r[:, :K_CHUNK] + c)
            if diag:
                p = jnp.where(causal, p, 0.0)
            vh = vc * hm_ref[h:h + 1, :]
            acc_ref[...] += jnp.dot(p.astype(BF16), vh, preferred_element_type=F32)
            carry_ref[h] = c + r[:, K_CHUNK:]

    def q_block(i, _):
        q0 = pl.multiple_of(i * Q_BLOCK, Q_BLOCK)
        q = q_ref[pl.ds(q0, Q_BLOCK), :]
        qh = [q * hm_ref[h:h + 1, :] for h in range(2)]
        acc_ref[...] = jnp.zeros_like(acc_ref)
        carry_ref[...] = jnp.zeros_like(carry_ref)
        chunk(qh, i, True)

        def left(t, _):
            chunk(qh, i - 1 - t, False)
            return 0

        lax.fori_loop(0, i, left, 0)

        o = acc_ref[...]
        o2 = o * o
        ms0 = jnp.sum(jnp.where(first_head, o2, 0.0), axis=1, keepdims=True) * (1.0 / HEAD_DIM)
        ms1 = jnp.sum(jnp.where(first_head, 0.0, o2), axis=1, keepdims=True) * (1.0 / HEAD_DIM)
        inv = jnp.where(first_head, lax.rsqrt(ms0 + EPS), lax.rsqrt(ms1 + EPS))
        y = o * inv * g_ref[...] * sg_ref[pl.ds(q0, Q_BLOCK), :]
        o_ref[pl.ds(q0, Q_BLOCK), :] = y.astype(BF16)
        return 0

    lax.fori_loop(0, n_blocks, q_block, 0)


def _cumsum_weights():
    j = lax.broadcasted_iota(jnp.int32, (K_CHUNK, K_CHUNK), 0)
    s = lax.broadcasted_iota(jnp.int32, (K_CHUNK, K_CHUNK), 1)
    neg_u = jnp.where(j >= s, -1.0, 0.0).astype(BF16)
    half = jnp.concatenate([neg_u, jnp.full((K_CHUNK, K_CHUNK), -1.0, BF16)], axis=1)
    return jnp.concatenate([half, half], axis=0)


def _head_masks():
    lane = lax.broadcasted_iota(jnp.int32, (2, LANES), 1)
    head = lax.broadcasted_iota(jnp.int32, (2, LANES), 0)
    return ((lane // HEAD_DIM) == head).astype(BF16)


def _attention(q, k, v, sg, g_pair, batch, seq):
    n_pairs = ATTN_DIM // LANES
    blk = lambda b, p: (b, p)
    seq_spec = pl.BlockSpec((seq, LANES), lambda b, p: (b, p))
    const = lambda b, p: (0, 0)
    return pl.pallas_call(
        _attn_kernel,
        grid=(batch, n_pairs),
        in_specs=[
            seq_spec, seq_spec, seq_spec, seq_spec,
            pl.BlockSpec((1, LANES), const),
            pl.BlockSpec((2, LANES), const),
            pl.BlockSpec((2 * K_CHUNK, 2 * K_CHUNK), const),
        ],
        out_specs=seq_spec,
        out_shape=jax.ShapeDtypeStruct((batch * seq, ATTN_DIM), BF16),
        scratch_shapes=[
            pltpu.VMEM((Q_BLOCK, LANES), F32),
            pltpu.VMEM((2, Q_BLOCK, K_CHUNK), F32),
        ],
        compiler_params=pltpu.CompilerParams(
            dimension_semantics=("arbitrary", "arbitrary"), vmem_limit_bytes=VMEM_LIMIT),
        name="sb_attention",
    )(q, k, v, sg, g_pair, _head_masks(), _cumsum_weights())


def _conv_kernel(c_ref, halo_ref, sg_ref, dww_ref, dwb_ref, lng_ref, lnb_ref, wpw_ref, og_ref,
                 o_ref, xw_ref, cv_ref):
    i = pl.program_id(1)
    halo = halo_ref[...]
    xw_ref[0:HALO, :] = jnp.where(i == 0, jnp.zeros_like(halo), halo)
    xw_ref[HALO:, :] = c_ref[...]

    base = HALO - (CONV_WIDTH - 1)
    for r0 in range(0, CONV_TILE, 128):
        for l0 in range(0, CONV_DIM, LANES):
            acc = jnp.zeros((128, LANES), F32) + dwb_ref[:, l0:l0 + LANES]
            for w in range(CONV_WIDTH):
                acc = acc + (xw_ref[r0 + base + w:r0 + base + w + 128, l0:l0 + LANES]
                             * dww_ref[w:w + 1, l0:l0 + LANES])
            cv_ref[r0:r0 + 128, l0:l0 + LANES] = acc

    x = cv_ref[...]
    mu = jnp.mean(x, axis=-1, keepdims=True)
    xc = x - mu
    var = jnp.mean(xc * xc, axis=-1, keepdims=True)
    y = xc * lax.rsqrt(var + EPS) * lng_ref[...] + lnb_ref[...]
    y = y * _sigmoid(y)
    z = jnp.dot(y.astype(BF16), wpw_ref[...], preferred_element_type=F32)
    ms = jnp.mean(z * z, axis=-1, keepdims=True)
    out = z * lax.rsqrt(ms + EPS) * og_ref[...] * sg_ref[...]
    o_ref[...] = out.astype(BF16)


def _conv_branch(c, sg, dw_w, dw_b, ln_g, ln_b, w_pw_bf16, out_g, batch, seq):
    tiles = seq // CONV_TILE
    halo_per_tile = CONV_TILE // HALO
    tile_spec = pl.BlockSpec((CONV_TILE, CONV_DIM), lambda b, i: (b * tiles + i, 0))
    halo_spec = pl.BlockSpec(
        (HALO, CONV_DIM),
        lambda b, i: (jnp.maximum((b * tiles + i) * halo_per_tile - 1, 0), 0))
    const = lambda b, i: (0, 0)
    vec = pl.BlockSpec((1, CONV_DIM), const)
    return pl.pallas_call(
        _conv_kernel,
        grid=(batch, tiles),
        in_specs=[
            tile_spec, halo_spec, tile_spec,
            pl.BlockSpec((CONV_WIDTH, CONV_DIM), const),
            vec, vec, vec,
            pl.BlockSpec((CONV_DIM, CONV_DIM), const),
            vec,
        ],
        out_specs=tile_spec,
        out_shape=jax.ShapeDtypeStruct((batch * seq, CONV_DIM), BF16),
        scratch_shapes=[
            pltpu.VMEM((CONV_TILE + HALO, CONV_DIM), F32),
            pltpu.VMEM((CONV_TILE, CONV_DIM), F32),
        ],
        compiler_params=pltpu.CompilerParams(
            dimension_semantics=("arbitrary", "arbitrary"), vmem_limit_bytes=VMEM_LIMIT),
        name="conv_branch",
    )(c, c, sg, dw_w, dw_b, ln_g, ln_b, w_pw_bf16, out_g)


def _out_proj_kernel(ya_ref, yc_ref, h_ref, p_ref, wa_ref, wc_ref, ng_ref, wg_ref, wp_ref, fg_ref,
                     o_ref, *, final):
    h = h_ref[...]
    h = h + jnp.dot(ya_ref[...], wa_ref[...], preferred_element_type=F32)
    h = h + jnp.dot(yc_ref[...], wc_ref[...], preferred_element_type=F32)
    ms = jnp.mean(h * h, axis=-1, keepdims=True)
    hn = (h * lax.rsqrt(ms + EPS) * ng_ref[...]).astype(BF16)
    gate = _sigmoid(jnp.dot(hn, wg_ref[...], preferred_element_type=F32))
    e = jnp.dot(p_ref[...].astype(BF16), wp_ref[...], preferred_element_type=F32)
    h = h + e * gate
    if final:
        ms = jnp.mean(h * h, axis=-1, keepdims=True)
        h = h * lax.rsqrt(ms + EPS) * fg_ref[...]
    o_ref[...] = h


def _out_proj(ya, yc, h2d, p2d, wa, wc, ng, wg, wp, fg, final):
    m = h2d.shape[0]
    row = lambda i: (i, 0)
    const = lambda i: (0, 0)
    return pl.pallas_call(
        functools.partial(_out_proj_kernel, final=final),
        grid=(m // ROW_TILE,),
        in_specs=[
            pl.BlockSpec((ROW_TILE, ATTN_DIM), row),
            pl.BlockSpec((ROW_TILE, CONV_DIM), row),
            pl.BlockSpec((ROW_TILE, D_MODEL), row),
            pl.BlockSpec((ROW_TILE, PLE_DIM), row),
            pl.BlockSpec((ATTN_DIM, D_MODEL), const),
            pl.BlockSpec((CONV_DIM, D_MODEL), const),
            pl.BlockSpec((1, D_MODEL), const),
            pl.BlockSpec((D_MODEL, D_MODEL), const),
            pl.BlockSpec((PLE_DIM, D_MODEL), const),
            pl.BlockSpec((1, D_MODEL), const),
        ],
        out_specs=pl.BlockSpec((ROW_TILE, D_MODEL), row),
        out_shape=jax.ShapeDtypeStruct((m, D_MODEL), F32),
        compiler_params=pltpu.CompilerParams(
            dimension_semantics=("arbitrary",), vmem_limit_bytes=VMEM_LIMIT),
        name="out_proj_final" if final else "out_proj",
    )(ya, yc, h2d, p2d, wa, wc, ng, wg, wp, fg)


def kernel(x, p, norm_g, w_in, attn_out_g, dw_w, dw_b, conv_ln_g, conv_ln_b, w_pw, conv_out_g,
           w_out, ple_norm_g, w_ple_gate, w_ple, final_g):
    batch, seq, _ = x.shape
    depth = w_in.shape[0]
    m = batch * seq
    h = x.reshape(m, D_MODEL)
    vec = lambda a: a.reshape(1, -1).astype(F32)
    for i in range(depth):
        q, k, v, sga, c, sgc = _in_proj(h, vec(norm_g[i]), w_in[i].astype(BF16))
        g_pair = vec(jnp.concatenate([attn_out_g[i], attn_out_g[i]]))
        ya = _attention(q, k, v, sga, g_pair, batch, seq)
        yc = _conv_branch(c, sgc, dw_w[i].astype(F32), vec(dw_b[i]), vec(conv_ln_g[i]),
                          vec(conv_ln_b[i]), w_pw[i].astype(BF16), vec(conv_out_g[i]), batch, seq)
        w_o = w_out[i].astype(BF16)
        h = _out_proj(ya, yc, h, p[i].reshape(m, PLE_DIM), w_o[:ATTN_DIM], w_o[ATTN_DIM:],
                      vec(ple_norm_g[i]), w_ple_gate[i].astype(BF16), w_ple[i].astype(BF16),
                      vec(final_g), final=(i == depth - 1))
    return h.reshape(batch, seq, D_MODEL)
```

```python
import functools
import math

import jax
import jax.numpy as jnp
from jax import lax
from jax.experimental import pallas as pl
from jax.experimental.pallas import tpu as pltpu

D_MODEL = 1024
ATTN_DIM = 512
CONV_DIM = 512
HEAD_DIM = 64
CONV_WIDTH = 31
PLE_DIM = 256
D_IN = 4 * ATTN_DIM + 3 * CONV_DIM
EPS = 1e-6

LANES = 128
Q_BLOCK = 256
K_CHUNK = 256
HALO = 32
ROW_TILE = 512
CONV_TILE = 256
VMEM_LIMIT = 56 * 1024 * 1024

F32 = jnp.float32
BF16 = jnp.bfloat16


def _sigmoid(x):
    return 1.0 / (1.0 + jnp.exp(-x))


def _in_proj_kernel(x_ref, g_ref, w_ref, q_ref, k_ref, v_ref, sga_ref, c_ref, sgc_ref):
    x = x_ref[...]
    ms = jnp.mean(x * x, axis=-1, keepdims=True)
    hn = (x * lax.rsqrt(ms + EPS) * g_ref[...]).astype(BF16)

    def seg(i):
        return jnp.dot(hn, w_ref[:, i * ATTN_DIM:(i + 1) * ATTN_DIM], preferred_element_type=F32)

    q_ref[...] = (seg(0) * (HEAD_DIM ** -0.5)).astype(BF16)
    k_ref[...] = seg(1).astype(BF16)
    v_ref[...] = seg(2).astype(BF16)
    ga = seg(3)
    sga_ref[...] = ga * _sigmoid(ga)
    c_ref[...] = seg(4) * _sigmoid(seg(5))
    gc = seg(6)
    sgc_ref[...] = gc * _sigmoid(gc)


def _in_proj(h2d, g, w_bf16):
    m = h2d.shape[0]
    row = lambda i: (i, 0)
    const = lambda i: (0, 0)
    seg_spec = pl.BlockSpec((ROW_TILE, ATTN_DIM), row)
    return pl.pallas_call(
        _in_proj_kernel,
        grid=(m // ROW_TILE,),
        in_specs=[
            pl.BlockSpec((ROW_TILE, D_MODEL), row),
            pl.BlockSpec((1, D_MODEL), const),
            pl.BlockSpec((D_MODEL, D_IN), const),
        ],
        out_specs=[seg_spec] * 6,
        out_shape=[
            jax.ShapeDtypeStruct((m, ATTN_DIM), BF16),
            jax.ShapeDtypeStruct((m, ATTN_DIM), BF16),
            jax.ShapeDtypeStruct((m, ATTN_DIM), BF16),
            jax.ShapeDtypeStruct((m, ATTN_DIM), F32),
            jax.ShapeDtypeStruct((m, CONV_DIM), F32),
            jax.ShapeDtypeStruct((m, CONV_DIM), F32),
        ],
        compiler_params=pltpu.CompilerParams(
            dimension_semantics=("arbitrary",), vmem_limit_bytes=VMEM_LIMIT),
        name="in_proj",
    )(h2d, g, w_bf16)


def _attn_kernel(q_ref, k_ref, v_ref, sg_ref, g_ref, hm_ref, w_ref, o_ref, acc_ref, carry_ref):
    n_blocks = q_ref.shape[0] // Q_BLOCK
    n_sub = K_CHUNK // LANES
    row = lax.broadcasted_iota(jnp.int32, (Q_BLOCK, K_CHUNK), 0)
    col = lax.broadcasted_iota(jnp.int32, (Q_BLOCK, K_CHUNK), 1)
    causal = col < row
    lane = lax.broadcasted_iota(jnp.int32, (Q_BLOCK, LANES), 1)
    first_head = lane < HEAD_DIM

    def chunk(qh, j, diag):
        k0 = pl.multiple_of(j * K_CHUNK, K_CHUNK)
        kc = k_ref[pl.ds(k0, K_CHUNK), :]
        vc = v_ref[pl.ds(k0, K_CHUNK), :]
        for h in range(2):
            z = lax.dot_general(qh[h], kc, (((1,), (1,)), ((), ())), preferred_element_type=F32)
            sp = jnp.maximum(z, 0.0) + jnp.log(1.0 + jnp.exp(-jnp.abs(z)))
            if diag:
                sp = jnp.where(causal, sp, 0.0)
            hi = sp.astype(BF16)
            lo = (sp - hi.astype(F32)).astype(BF16)
            c = carry_ref[h]
            ps = [None] * n_sub
            for s in reversed(range(n_sub)):
                sl = slice(s * LANES, (s + 1) * LANES)
                r = jnp.dot(jnp.concatenate([hi[:, sl], lo[:, sl]], axis=1), w_ref[...],
                            preferred_element_type=F32)
                ps[s] = jnp.exp(z[:, sl] + r[:, :LANES] + c)
                c = c + r[:, LANES:]
            p = jnp.concatenate(ps, axis=1)
            if diag:
                p = jnp.where(causal, p, 0.0)
            vh = vc * hm_ref[h:h + 1, :]
            acc_ref[...] += jnp.dot(p.astype(BF16), vh, preferred_element_type=F32)
            carry_ref[h] = c

    def q_block(i, _):
        q0 = pl.multiple_of(i * Q_BLOCK, Q_BLOCK)
        q = q_ref[pl.ds(q0, Q_BLOCK), :]
        qh = [q * hm_ref[h:h + 1, :] for h in range(2)]
        acc_ref[...] = jnp.zeros_like(acc_ref)
        carry_ref[...] = jnp.zeros_like(carry_ref)
        chunk(qh, i, True)

        def left(t, _):
            chunk(qh, i - 1 - t, False)
            return 0

        lax.fori_loop(0, i, left, 0)

        o = acc_ref[...]
        o2 = o * o
        ms0 = jnp.sum(jnp.where(first_head, o2, 0.0), axis=1, keepdims=True) * (1.0 / HEAD_DIM)
        ms1 = jnp.sum(jnp.where(first_head, 0.0, o2), axis=1, keepdims=True) * (1.0 / HEAD_DIM)
        inv = jnp.where(first_head, lax.rsqrt(ms0 + EPS), lax.rsqrt(ms1 + EPS))
        y = o * inv * g_ref[...] * sg_ref[pl.ds(q0, Q_BLOCK), :]
        o_ref[pl.ds(q0, Q_BLOCK), :] = y.astype(BF16)
        return 0

    lax.fori_loop(0, n_blocks, q_block, 0)


def _cumsum_weights():
    j = lax.broadcasted_iota(jnp.int32, (LANES, LANES), 0)
    s = lax.broadcasted_iota(jnp.int32, (LANES, LANES), 1)
    neg_u = jnp.where(j >= s, -1.0, 0.0).astype(BF16)
    half = jnp.concatenate([neg_u, jnp.full((LANES, LANES), -1.0, BF16)], axis=1)
    return jnp.concatenate([half, half], axis=0)


def _head_masks():
    lane = lax.broadcasted_iota(jnp.int32, (2, LANES), 1)
    head = lax.broadcasted_iota(jnp.int32, (2, LANES), 0)
    return ((lane // HEAD_DIM) == head).astype(BF16)


def _attention(q, k, v, sg, g_pair, batch, seq):
    n_pairs = ATTN_DIM // LANES
    blk = lambda b, p: (b, p)
    seq_spec = pl.BlockSpec((seq, LANES), lambda b, p: (b, p))
    const = lambda b, p: (0, 0)
    return pl.pallas_call(
        _attn_kernel,
        grid=(batch, n_pairs),
        in_specs=[
            seq_spec, seq_spec, seq_spec, seq_spec,
            pl.BlockSpec((1, LANES), const),
            pl.BlockSpec((2, LANES), const),
            pl.BlockSpec((2 * LANES, 2 * LANES), const),
        ],
        out_specs=seq_spec,
        out_shape=jax.ShapeDtypeStruct((batch * seq, ATTN_DIM), BF16),
        scratch_shapes=[
            pltpu.VMEM((Q_BLOCK, LANES), F32),
            pltpu.VMEM((2, Q_BLOCK, LANES), F32),
        ],
        compiler_params=pltpu.CompilerParams(
            dimension_semantics=("arbitrary", "arbitrary"), vmem_limit_bytes=VMEM_LIMIT),
        name="sb_attention",
    )(q, k, v, sg, g_pair, _head_masks(), _cumsum_weights())


def _conv_kernel(c_ref, halo_ref, sg_ref, dww_ref, dwb_ref, lng_ref, lnb_ref, wpw_ref, og_ref,
                 o_ref, xw_ref, cv_ref):
    i = pl.program_id(1)
    halo = halo_ref[...]
    xw_ref[0:HALO, :] = jnp.where(i == 0, jnp.zeros_like(halo), halo)
    xw_ref[HALO:, :] = c_ref[...]

    base = HALO - (CONV_WIDTH - 1)
    for r0 in range(0, CONV_TILE, 128):
        for l0 in range(0, CONV_DIM, LANES):
            acc = jnp.zeros((128, LANES), F32) + dwb_ref[:, l0:l0 + LANES]
            for w in range(CONV_WIDTH):
                acc = acc + (xw_ref[r0 + base + w:r0 + base + w + 128, l0:l0 + LANES]
                             * dww_ref[w:w + 1, l0:l0 + LANES])
            cv_ref[r0:r0 + 128, l0:l0 + LANES] = acc

    x = cv_ref[...]
    mu = jnp.mean(x, axis=-1, keepdims=True)
    xc = x - mu
    var = jnp.mean(xc * xc, axis=-1, keepdims=True)
    y = xc * lax.rsqrt(var + EPS) * lng_ref[...] + lnb_ref[...]
    y = y * _sigmoid(y)
    z = jnp.dot(y.astype(BF16), wpw_ref[...], preferred_element_type=F32)
    ms = jnp.mean(z * z, axis=-1, keepdims=True)
    out = z * lax.rsqrt(ms + EPS) * og_ref[...] * sg_ref[...]
    o_ref[...] = out.astype(BF16)


def _conv_branch(c, sg, dw_w, dw_b, ln_g, ln_b, w_pw_bf16, out_g, batch, seq):
    tiles = seq // CONV_TILE
    halo_per_tile = CONV_TILE // HALO
    tile_spec = pl.BlockSpec((CONV_TILE, CONV_DIM), lambda b, i: (b * tiles + i, 0))
    halo_spec = pl.BlockSpec(
        (HALO, CONV_DIM),
        lambda b, i: (jnp.maximum((b * tiles + i) * halo_per_tile - 1, 0), 0))
    const = lambda b, i: (0, 0)
    vec = pl.BlockSpec((1, CONV_DIM), const)
    return pl.pallas_call(
        _conv_kernel,
        grid=(batch, tiles),
        in_specs=[
            tile_spec, halo_spec, tile_spec,
            pl.BlockSpec((CONV_WIDTH, CONV_DIM), const),
            vec, vec, vec,
            pl.BlockSpec((CONV_DIM, CONV_DIM), const),
            vec,
        ],
        out_specs=tile_spec,
        out_shape=jax.ShapeDtypeStruct((batch * seq, CONV_DIM), BF16),
        scratch_shapes=[
            pltpu.VMEM((CONV_TILE + HALO, CONV_DIM), F32),
            pltpu.VMEM((CONV_TILE, CONV_DIM), F32),
        ],
        compiler_params=pltpu.CompilerParams(
            dimension_semantics=("arbitrary", "arbitrary"), vmem_limit_bytes=VMEM_LIMIT),
        name="conv_branch",
    )(c, c, sg, dw_w, dw_b, ln_g, ln_b, w_pw_bf16, out_g)


def _out_proj_kernel(ya_ref, yc_ref, h_ref, p_ref, wa_ref, wc_ref, ng_ref, wg_ref, wp_ref, fg_ref,
                     o_ref, *, final):
    h = h_ref[...]
    h = h + jnp.dot(ya_ref[...], wa_ref[...], preferred_element_type=F32)
    h = h + jnp.dot(yc_ref[...], wc_ref[...], preferred_element_type=F32)
    ms = jnp.mean(h * h, axis=-1, keepdims=True)
    hn = (h * lax.rsqrt(ms + EPS) * ng_ref[...]).astype(BF16)
    gate = _sigmoid(jnp.dot(hn, wg_ref[...], preferred_element_type=F32))
    e = jnp.dot(p_ref[...].astype(BF16), wp_ref[...], preferred_element_type=F32)
    h = h + e * gate
    if final:
        ms = jnp.mean(h * h, axis=-1, keepdims=True)
        h = h * lax.rsqrt(ms + EPS) * fg_ref[...]
    o_ref[...] = h


def _out_proj(ya, yc, h2d, p2d, wa, wc, ng, wg, wp, fg, final):
    m = h2d.shape[0]
    row = lambda i: (i, 0)
    const = lambda i: (0, 0)
    return pl.pallas_call(
        functools.partial(_out_proj_kernel, final=final),
        grid=(m // ROW_TILE,),
        in_specs=[
            pl.BlockSpec((ROW_TILE, ATTN_DIM), row),
            pl.BlockSpec((ROW_TILE, CONV_DIM), row),
            pl.BlockSpec((ROW_TILE, D_MODEL), row),
            pl.BlockSpec((ROW_TILE, PLE_DIM), row),
            pl.BlockSpec((ATTN_DIM, D_MODEL), const),
            pl.BlockSpec((CONV_DIM, D_MODEL), const),
            pl.BlockSpec((1, D_MODEL), const),
            pl.BlockSpec((D_MODEL, D_MODEL), const),
            pl.BlockSpec((PLE_DIM, D_MODEL), const),
            pl.BlockSpec((1, D_MODEL), const),
        ],
        out_specs=pl.BlockSpec((ROW_TILE, D_MODEL), row),
        out_shape=jax.ShapeDtypeStruct((m, D_MODEL), F32),
        compiler_params=pltpu.CompilerParams(
            dimension_semantics=("arbitrary",), vmem_limit_bytes=VMEM_LIMIT),
        name="out_proj_final" if final else "out_proj",
    )(ya, yc, h2d, p2d, wa, wc, ng, wg, wp, fg)


def kernel(x, p, norm_g, w_in, attn_out_g, dw_w, dw_b, conv_ln_g, conv_ln_b, w_pw, conv_out_g,
           w_out, ple_norm_g, w_ple_gate, w_ple, final_g):
    batch, seq, _ = x.shape
    depth = w_in.shape[0]
    m = batch * seq
    h = x.reshape(m, D_MODEL)
    vec = lambda a: a.reshape(1, -1).astype(F32)
    for i in range(depth):
        q, k, v, sga, c, sgc = _in_proj(h, vec(norm_g[i]), w_in[i].astype(BF16))
        g_pair = vec(jnp.concatenate([attn_out_g[i], attn_out_g[i]]))
        ya = _attention(q, k, v, sga, g_pair, batch, seq)
        yc = _conv_branch(c, sgc, dw_w[i].astype(F32), vec(dw_b[i]), vec(conv_ln_g[i]),
                          vec(conv_ln_b[i]), w_pw[i].astype(BF16), vec(conv_out_g[i]), batch, seq)
        w_o = w_out[i].astype(BF16)
        h = _out_proj(ya, yc, h, p[i].reshape(m, PLE_DIM), w_o[:ATTN_DIM], w_o[ATTN_DIM:],
                      vec(ple_norm_g[i]), w_ple_gate[i].astype(BF16), w_ple[i].astype(BF16),
                      vec(final_g), final=(i == depth - 1))
    return h.reshape(batch, seq, D_MODEL)
```

```python
import functools

import jax
import jax.numpy as jnp
from jax import lax
from jax.experimental import pallas as pl
from jax.experimental.pallas import tpu as pltpu

D_MODEL = 1024
ATTN_DIM = 512
CONV_DIM = 512
HEAD_DIM = 64
CONV_WIDTH = 31
PLE_DIM = 256
D_IN = 4 * ATTN_DIM + 3 * CONV_DIM
EPS = 1e-6

LANES = 128
SUBLANES = 8
K_CHUNK = 256
HALO = 32
ROW_TILE = 512
CONV_TILE = 256
CONV_ROWS = 128
VMEM_LIMIT = 56 * 1024 * 1024

F32 = jnp.float32
BF16 = jnp.bfloat16


def _sigmoid(x):
    return 1.0 / (1.0 + jnp.exp(-x))


def _in_proj_kernel(x_ref, g_ref, w_ref, q_ref, k_ref, v_ref, sga_ref, c_ref, sgc_ref):
    x = x_ref[...]
    ms = jnp.mean(x * x, axis=-1, keepdims=True)
    hn = (x * lax.rsqrt(ms + EPS) * g_ref[...]).astype(BF16)

    def seg(i):
        return jnp.dot(hn, w_ref[:, i * ATTN_DIM:(i + 1) * ATTN_DIM], preferred_element_type=F32)

    q_ref[...] = (seg(0) * (HEAD_DIM ** -0.5)).astype(BF16)
    k_ref[...] = seg(1).astype(BF16)
    v_ref[...] = seg(2).astype(BF16)
    ga = seg(3)
    sga_ref[...] = ga * _sigmoid(ga)
    c_ref[...] = seg(4) * _sigmoid(seg(5))
    gc = seg(6)
    sgc_ref[...] = gc * _sigmoid(gc)


def _in_proj(h2d, g, w_bf16):
    m = h2d.shape[0]
    row = lambda i: (i, 0)
    const = lambda i: (0, 0)
    seg_spec = pl.BlockSpec((ROW_TILE, ATTN_DIM), row)
    return pl.pallas_call(
        _in_proj_kernel,
        grid=(m // ROW_TILE,),
        in_specs=[
            pl.BlockSpec((ROW_TILE, D_MODEL), row),
            pl.BlockSpec((1, D_MODEL), const),
            pl.BlockSpec((D_MODEL, D_IN), const),
        ],
        out_specs=[seg_spec] * 6,
        out_shape=[
            jax.ShapeDtypeStruct((m, ATTN_DIM), BF16),
            jax.ShapeDtypeStruct((m, ATTN_DIM), BF16),
            jax.ShapeDtypeStruct((m, ATTN_DIM), BF16),
            jax.ShapeDtypeStruct((m, ATTN_DIM), F32),
            jax.ShapeDtypeStruct((m, CONV_DIM), F32),
            jax.ShapeDtypeStruct((m, CONV_DIM), F32),
        ],
        compiler_params=pltpu.CompilerParams(
            dimension_semantics=("arbitrary",), vmem_limit_bytes=VMEM_LIMIT),
        name="in_proj",
    )(h2d, g, w_bf16)


def _attn_kernel(q_ref, k_ref, v_ref, sg_ref, g_ref, hm_ref, w_ref, o_ref,
                 qm_ref, vm_ref, acc_ref, carry_ref):
    seq = q_ref.shape[0]
    n_chunks = seq // K_CHUNK
    n_sub = K_CHUNK // LANES
    row = lax.broadcasted_iota(jnp.int32, (K_CHUNK, K_CHUNK), 0)
    col = lax.broadcasted_iota(jnp.int32, (K_CHUNK, K_CHUNK), 1)
    causal = col < row

    for h in range(2):
        qm_ref[h] = q_ref[...] * hm_ref[h:h + 1, :]
        for j in range(n_chunks):
            vm_ref[j, h * K_CHUNK:(h + 1) * K_CHUNK, :] = (
                v_ref[j * K_CHUNK:(j + 1) * K_CHUNK, :] * hm_ref[h:h + 1, :])

    def on_rows(diag_fn, rest_fn, m):
        if m == K_CHUNK:
            return diag_fn(slice(0, K_CHUNK))
        return jnp.concatenate([diag_fn(slice(0, K_CHUNK)), rest_fn(slice(K_CHUNK, m))], axis=0)

    def sweep_step(j):
        r0 = j * K_CHUNK
        m = seq - r0
        kc = k_ref[r0:r0 + K_CHUNK, :]
        zs, lhs = [], []
        for h in range(2):
            z = lax.dot_general(qm_ref[h, r0:, :], kc, (((1,), (1,)), ((), ())),
                                preferred_element_type=F32)
            sp = jnp.maximum(z, 0.0) + jnp.log(1.0 + jnp.exp(-jnp.abs(z)))
            sp = on_rows(lambda d: jnp.where(causal, sp[d], 0.0), lambda b: sp[b], m)
            hi = sp.astype(BF16)
            lo = (sp - hi.astype(F32)).astype(BF16)
            zs.append(z)
            for s in reversed(range(n_sub)):
                sl = slice(s * LANES, (s + 1) * LANES)
                lhs.append(jnp.concatenate([hi[:, sl], lo[:, sl]], axis=1))
        r = jnp.dot(jnp.concatenate(lhs, axis=0), w_ref[...], preferred_element_type=F32)
        ps = []
        for h in range(2):
            z = zs[h]
            c = None
            p_sub = [None] * n_sub
            for idx, s in enumerate(reversed(range(n_sub))):
                sl = slice(s * LANES, (s + 1) * LANES)
                rr = r[(h * n_sub + idx) * m:(h * n_sub + idx + 1) * m]
                arg = z[:, sl] + rr[:, :LANES]
                if c is None:
                    c_in = carry_ref[h, r0 + K_CHUNK:, :] if m > K_CHUNK else None
                    arg = on_rows(lambda d: arg[d], lambda b: arg[b] + c_in, m)
                    c = on_rows(lambda d: rr[d, LANES:], lambda b: c_in + rr[b, LANES:], m)
                else:
                    arg = arg + c
                    c = c + rr[:, LANES:]
                p_sub[s] = jnp.exp(arg)
            carry_ref[h, r0:, :] = c
            p = jnp.concatenate(p_sub, axis=1)
            p = on_rows(lambda d: jnp.where(causal, p[d], 0.0), lambda b: p[b], m)
            ps.append(p.astype(BF16))
        pv = jnp.dot(jnp.concatenate(ps, axis=1), vm_ref[j], preferred_element_type=F32)
        acc_ref[r0:r0 + K_CHUNK, :] = pv[:K_CHUNK]
        if m > K_CHUNK:
            acc_ref[r0 + K_CHUNK:, :] += pv[K_CHUNK:]

    always = pl.program_id(0) >= 0
    for j in reversed(range(n_chunks)):
        pl.when(always)(functools.partial(sweep_step, j))

    lane = lax.broadcasted_iota(jnp.int32, (K_CHUNK, LANES), 1)
    first_head = lane < HEAD_DIM
    for i in range(n_chunks):
        rows = slice(i * K_CHUNK, (i + 1) * K_CHUNK)
        o = acc_ref[rows, :]
        o2 = o * o
        ms0 = jnp.sum(jnp.where(first_head, o2, 0.0), axis=1, keepdims=True) * (1.0 / HEAD_DIM)
        ms1 = jnp.sum(jnp.where(first_head, 0.0, o2), axis=1, keepdims=True) * (1.0 / HEAD_DIM)
        inv = jnp.where(first_head, lax.rsqrt(ms0 + EPS), lax.rsqrt(ms1 + EPS))
        y = o * inv * g_ref[...] * sg_ref[rows, :]
        o_ref[rows, :] = y.astype(BF16)


def _cumsum_weights():
    j = lax.broadcasted_iota(jnp.int32, (LANES, LANES), 0)
    s = lax.broadcasted_iota(jnp.int32, (LANES, LANES), 1)
    neg_u = jnp.where(j >= s, -1.0, 0.0).astype(BF16)
    half = jnp.concatenate([neg_u, jnp.full((LANES, LANES), -1.0, BF16)], axis=1)
    return jnp.concatenate([half, half], axis=0)


def _head_masks():
    lane = lax.broadcasted_iota(jnp.int32, (2, LANES), 1)
    head = lax.broadcasted_iota(jnp.int32, (2, LANES), 0)
    return ((lane // HEAD_DIM) == head).astype(BF16)


def _attention(q, k, v, sg, g_pair, batch, seq):
    n_pairs = ATTN_DIM // LANES
    seq_spec = pl.BlockSpec((seq, LANES), lambda b, p: (b, p))
    const = lambda b, p: (0, 0)
    return pl.pallas_call(
        _attn_kernel,
        grid=(batch, n_pairs),
        in_specs=[
            seq_spec, seq_spec, seq_spec, seq_spec,
            pl.BlockSpec((1, LANES), const),
            pl.BlockSpec((2, LANES), const),
            pl.BlockSpec((2 * LANES, 2 * LANES), const),
        ],
        out_specs=seq_spec,
        out_shape=jax.ShapeDtypeStruct((batch * seq, ATTN_DIM), BF16),
        scratch_shapes=[
            pltpu.VMEM((2, seq, LANES), BF16),
            pltpu.VMEM((seq // K_CHUNK, 2 * K_CHUNK, LANES), BF16),
            pltpu.VMEM((seq, LANES), F32),
            pltpu.VMEM((2, seq, LANES), F32),
        ],
        compiler_params=pltpu.CompilerParams(
            dimension_semantics=("arbitrary", "arbitrary"), vmem_limit_bytes=VMEM_LIMIT),
        name="sb_attention",
    )(q, k, v, sg, g_pair, _head_masks(), _cumsum_weights())


def _conv_kernel(c_ref, halo_ref, sg_ref, dww_ref, dwb_ref, lng_ref, lnb_ref, wpw_ref, og_ref,
                 o_ref, xw_ref, cv_ref):
    i = pl.program_id(1)
    halo = halo_ref[...]
    xw_ref[0:HALO, :] = jnp.where(i == 0, jnp.zeros_like(halo), halo)
    xw_ref[HALO:, :] = c_ref[...]

    base = HALO - (CONV_WIDTH - 1)
    for r0 in range(0, CONV_TILE, CONV_ROWS):
        for l0 in range(0, CONV_DIM, LANES):
            acc = jnp.zeros((CONV_ROWS, LANES), F32) + dwb_ref[:, l0:l0 + LANES]
            for phase in range(SUBLANES):
                taps = [w for w in range(CONV_WIDTH) if (base + w) % SUBLANES == phase]
                if not taps:
                    continue
                span = base + taps[-1] - phase + CONV_ROWS
                y = xw_ref[r0 + phase:r0 + phase + span, l0:l0 + LANES]
                for w in taps:
                    a = base + w - phase
                    acc = acc + y[a:a + CONV_ROWS, :] * dww_ref[w:w + 1, l0:l0 + LANES]
            cv_ref[r0:r0 + CONV_ROWS, l0:l0 + LANES] = acc

    x = cv_ref[...]
    mu = jnp.mean(x, axis=-1, keepdims=True)
    xc = x - mu
    var = jnp.mean(xc * xc, axis=-1, keepdims=True)
    y = xc * lax.rsqrt(var + EPS) * lng_ref[...] + lnb_ref[...]
    y = y * _sigmoid(y)
    z = jnp.dot(y.astype(BF16), wpw_ref[...], preferred_element_type=F32)
    ms = jnp.mean(z * z, axis=-1, keepdims=True)
    out = z * lax.rsqrt(ms + EPS) * og_ref[...] * sg_ref[...]
    o_ref[...] = out.astype(BF16)


def _conv_branch(c, sg, dw_w, dw_b, ln_g, ln_b, w_pw_bf16, out_g, batch, seq):
    tiles = seq // CONV_TILE
    halo_per_tile = CONV_TILE // HALO
    tile_spec = pl.BlockSpec((CONV_TILE, CONV_DIM), lambda b, i: (b * tiles + i, 0))
    halo_spec = pl.BlockSpec(
        (HALO, CONV_DIM),
        lambda b, i: (jnp.maximum((b * tiles + i) * halo_per_tile - 1, 0), 0))
    const = lambda b, i: (0, 0)
    vec = pl.BlockSpec((1, CONV_DIM), const)
    return pl.pallas_call(
        _conv_kernel,
        grid=(batch, tiles),
        in_specs=[
            tile_spec, halo_spec, tile_spec,
            pl.BlockSpec((CONV_WIDTH, CONV_DIM), const),
            vec, vec, vec,
            pl.BlockSpec((CONV_DIM, CONV_DIM), const),
            vec,
        ],
        out_specs=tile_spec,
        out_shape=jax.ShapeDtypeStruct((batch * seq, CONV_DIM), BF16),
        scratch_shapes=[
            pltpu.VMEM((CONV_TILE + HALO, CONV_DIM), F32),
            pltpu.VMEM((CONV_TILE, CONV_DIM), F32),
        ],
        compiler_params=pltpu.CompilerParams(
            dimension_semantics=("arbitrary", "arbitrary"), vmem_limit_bytes=VMEM_LIMIT),
        name="conv_branch",
    )(c, c, sg, dw_w, dw_b, ln_g, ln_b, w_pw_bf16, out_g)


def _out_proj_kernel(ya_ref, yc_ref, h_ref, p_ref, wa_ref, wc_ref, ng_ref, wg_ref, wp_ref, fg_ref,
                     o_ref, *, final):
    h = h_ref[...]
    h = h + jnp.dot(ya_ref[...], wa_ref[...], preferred_element_type=F32)
    h = h + jnp.dot(yc_ref[...], wc_ref[...], preferred_element_type=F32)
    ms = jnp.mean(h * h, axis=-1, keepdims=True)
    hn = (h * lax.rsqrt(ms + EPS) * ng_ref[...]).astype(BF16)
    gate = _sigmoid(jnp.dot(hn, wg_ref[...], preferred_element_type=F32))
    e = jnp.dot(p_ref[...].astype(BF16), wp_ref[...], preferred_element_type=F32)
    h = h + e * gate
    if final:
        ms = jnp.mean(h * h, axis=-1, keepdims=True)
        h = h * lax.rsqrt(ms + EPS) * fg_ref[...]
    o_ref[...] = h


def _out_proj(ya, yc, h2d, p2d, wa, wc, ng, wg, wp, fg, final):
    m = h2d.shape[0]
    row = lambda i: (i, 0)
    const = lambda i: (0, 0)
    return pl.pallas_call(
        functools.partial(_out_proj_kernel, final=final),
        grid=(m // ROW_TILE,),
        in_specs=[
            pl.BlockSpec((ROW_TILE, ATTN_DIM), row),
            pl.BlockSpec((ROW_TILE, CONV_DIM), row),
            pl.BlockSpec((ROW_TILE, D_MODEL), row),
            pl.BlockSpec((ROW_TILE, PLE_DIM), row),
            pl.BlockSpec((ATTN_DIM, D_MODEL), const),
            pl.BlockSpec((CONV_DIM, D_MODEL), const),
            pl.BlockSpec((1, D_MODEL), const),
            pl.BlockSpec((D_MODEL, D_MODEL), const),
            pl.BlockSpec((PLE_DIM, D_MODEL), const),
            pl.BlockSpec((1, D_MODEL), const),
        ],
        out_specs=pl.BlockSpec((ROW_TILE, D_MODEL), row),
        out_shape=jax.ShapeDtypeStruct((m, D_MODEL), F32),
        compiler_params=pltpu.CompilerParams(
            dimension_semantics=("arbitrary",), vmem_limit_bytes=VMEM_LIMIT),
        name="out_proj_final" if final else "out_proj",
    )(ya, yc, h2d, p2d, wa, wc, ng, wg, wp, fg)


def kernel(x, p, norm_g, w_in, attn_out_g, dw_w, dw_b, conv_ln_g, conv_ln_b, w_pw, conv_out_g,
           w_out, ple_norm_g, w_ple_gate, w_ple, final_g):
    batch, seq, _ = x.shape
    depth = w_in.shape[0]
    m = batch * seq
    h = x.reshape(m, D_MODEL)
    vec = lambda a: a.reshape(1, -1).astype(F32)
    for i in range(depth):
        q, k, v, sga, c, sgc = _in_proj(h, vec(norm_g[i]), w_in[i].astype(BF16))
        g_pair = vec(jnp.concatenate([attn_out_g[i], attn_out_g[i]]))
        ya = _attention(q, k, v, sga, g_pair, batch, seq)
        yc = _conv_branch(c, sgc, dw_w[i].astype(F32), vec(dw_b[i]), vec(conv_ln_g[i]),
                          vec(conv_ln_b[i]), w_pw[i].astype(BF16), vec(conv_out_g[i]), batch, seq)
        w_o = w_out[i].astype(BF16)
        h = _out_proj(ya, yc, h, p[i].reshape(m, PLE_DIM), w_o[:ATTN_DIM], w_o[ATTN_DIM:],
                      vec(ple_norm_g[i]), w_ple_gate[i].astype(BF16), w_ple[i].astype(BF16),
                      vec(final_g), final=(i == depth - 1))
    return h.reshape(batch, seq, D_MODEL)
```

```python
import functools

import jax
import jax.numpy as jnp
from jax import lax
from jax.experimental import pallas as pl
from jax.experimental.pallas import tpu as pltpu

D_MODEL = 1024
ATTN_DIM = 512
CONV_DIM = 512
HEAD_DIM = 64
CONV_WIDTH = 31
PLE_DIM = 256
D_IN = 4 * ATTN_DIM + 3 * CONV_DIM
EPS = 1e-6

LANES = 128
SUBLANES = 8
K_CHUNK = 256
HALO = 32
ROW_TILE = 512
CONV_TILE = 256
CONV_ROWS = 128
VMEM_LIMIT = 56 * 1024 * 1024
LOG2E = 1.4426950408889634
EXP_UNDERFLOW = -104.0

F32 = jnp.float32
BF16 = jnp.bfloat16


def _sigmoid(x):
    return 1.0 / (1.0 + jnp.exp(-x))


def _in_proj_kernel(x_ref, g_ref, w_ref, q_ref, k_ref, v_ref, sga_ref, c_ref, sgc_ref):
    x = x_ref[...]
    ms = jnp.mean(x * x, axis=-1, keepdims=True)
    hn = (x * lax.rsqrt(ms + EPS) * g_ref[...]).astype(BF16)

    def seg(i):
        return jnp.dot(hn, w_ref[:, i * ATTN_DIM:(i + 1) * ATTN_DIM], preferred_element_type=F32)

    q_ref[...] = (seg(0) * (HEAD_DIM ** -0.5)).astype(BF16)
    k_ref[...] = seg(1).astype(BF16)
    v_ref[...] = seg(2).astype(BF16)
    ga = seg(3)
    sga_ref[...] = ga * _sigmoid(ga)
    c_ref[...] = seg(4) * _sigmoid(seg(5))
    gc = seg(6)
    sgc_ref[...] = gc * _sigmoid(gc)


def _in_proj(h2d, g, w_bf16):
    m = h2d.shape[0]
    row = lambda i: (i, 0)
    const = lambda i: (0, 0)
    seg_spec = pl.BlockSpec((ROW_TILE, ATTN_DIM), row)
    return pl.pallas_call(
        _in_proj_kernel,
        grid=(m // ROW_TILE,),
        in_specs=[
            pl.BlockSpec((ROW_TILE, D_MODEL), row),
            pl.BlockSpec((1, D_MODEL), const),
            pl.BlockSpec((D_MODEL, D_IN), const),
        ],
        out_specs=[seg_spec] * 6,
        out_shape=[
            jax.ShapeDtypeStruct((m, ATTN_DIM), BF16),
            jax.ShapeDtypeStruct((m, ATTN_DIM), BF16),
            jax.ShapeDtypeStruct((m, ATTN_DIM), BF16),
            jax.ShapeDtypeStruct((m, ATTN_DIM), F32),
            jax.ShapeDtypeStruct((m, CONV_DIM), F32),
            jax.ShapeDtypeStruct((m, CONV_DIM), F32),
        ],
        compiler_params=pltpu.CompilerParams(
            dimension_semantics=("arbitrary",), vmem_limit_bytes=VMEM_LIMIT),
        name="in_proj",
    )(h2d, g, w_bf16)


def _attn_kernel(q_ref, k_ref, v_ref, sg_ref, g_ref, hm_ref, w_ref, o_ref,
                 qm_ref, vm_ref, acc_ref, carry_ref):
    seq = q_ref.shape[0]
    n_chunks = seq // K_CHUNK
    n_sub = K_CHUNK // LANES
    row = lax.broadcasted_iota(jnp.int32, (K_CHUNK, K_CHUNK), 0)
    col = lax.broadcasted_iota(jnp.int32, (K_CHUNK, K_CHUNK), 1)
    causal = col < row

    for h in range(2):
        qm_ref[h] = q_ref[...] * hm_ref[h:h + 1, :]
        for j in range(n_chunks):
            vm_ref[j, h * K_CHUNK:(h + 1) * K_CHUNK, :] = (
                v_ref[j * K_CHUNK:(j + 1) * K_CHUNK, :] * hm_ref[h:h + 1, :])

    def score(q0, m, j, has_diag):
        n_diag = K_CHUNK if has_diag else 0

        def on_rows(diag_fn, rest_fn):
            parts = [diag_fn(slice(0, n_diag))] if has_diag else []
            if m > n_diag:
                parts.append(rest_fn(slice(n_diag, m)))
            return parts[0] if len(parts) == 1 else jnp.concatenate(parts, axis=0)

        kc = k_ref[pl.ds(j * K_CHUNK, K_CHUNK), :]
        zs, lhs = [], []
        for h in range(2):
            z = lax.dot_general(qm_ref[h, pl.ds(q0, m), :], kc, (((1,), (1,)), ((), ())),
                                preferred_element_type=F32)
            sp = jnp.maximum(z, 0.0) + jnp.log(1.0 + jnp.exp2(jnp.abs(z) * -LOG2E))
            sp = on_rows(lambda d: jnp.where(causal, sp[d], 0.0), lambda b: sp[b])
            hi = sp.astype(BF16)
            lo = (sp - hi.astype(F32)).astype(BF16)
            zs.append(z)
            for s in reversed(range(n_sub)):
                sl = slice(s * LANES, (s + 1) * LANES)
                lhs.append(jnp.concatenate([hi[:, sl], lo[:, sl]], axis=1))
        r = jnp.dot(jnp.concatenate(lhs, axis=0), w_ref[...], preferred_element_type=F32)
        ps = []
        for h in range(2):
            z = zs[h]
            c = None
            p_sub = [None] * n_sub
            for idx, s in enumerate(reversed(range(n_sub))):
                sl = slice(s * LANES, (s + 1) * LANES)
                rr = r[(h * n_sub + idx) * m:(h * n_sub + idx + 1) * m]
                arg = z[:, sl] + rr[:, :LANES]
                if c is None:
                    c_in = carry_ref[h, pl.ds(q0 + n_diag, m - n_diag), :] if m > n_diag else None
                    arg = on_rows(lambda d: arg[d], lambda b: arg[b] + c_in)
                    c = on_rows(lambda d: rr[d, LANES:], lambda b: c_in + rr[b, LANES:])
                else:
                    arg = arg + c
                    c = c + rr[:, LANES:]
                p_sub[s] = jnp.exp(arg)
            carry_ref[h, pl.ds(q0, m), :] = c
            p = jnp.concatenate(p_sub, axis=1)
            p = on_rows(lambda d: jnp.where(causal, p[d], 0.0), lambda b: p[b])
            ps.append(p.astype(BF16))
        pv = jnp.dot(jnp.concatenate(ps, axis=1), vm_ref[j], preferred_element_type=F32)
        if has_diag:
            acc_ref[pl.ds(q0, n_diag), :] = pv[:n_diag]
        if m > n_diag:
            acc_ref[pl.ds(q0 + n_diag, m - n_diag), :] += pv[n_diag:]

    for j in reversed(range(n_chunks)):
        score(j * K_CHUNK, min(2 * K_CHUNK, seq - j * K_CHUNK), j, True)

    def finish_block(i, _):
        q0 = pl.multiple_of(i * K_CHUNK, K_CHUNK)

        def alive():
            return jnp.max(carry_ref[:, pl.ds(q0, K_CHUNK), :]) > EXP_UNDERFLOW

        def more(state):
            j, live = state
            return jnp.logical_and(j >= 0, live)

        def step(state):
            j, _ = state
            score(q0, K_CHUNK, j, False)
            return j - 1, alive()

        lax.while_loop(more, step, (i - 2, alive()))
        return 0

    lax.fori_loop(2, n_chunks, finish_block, 0)

    lane = lax.broadcasted_iota(jnp.int32, (K_CHUNK, LANES), 1)
    first_head = lane < HEAD_DIM
    for i in range(n_chunks):
        rows = slice(i * K_CHUNK, (i + 1) * K_CHUNK)
        o = acc_ref[rows, :]
        o2 = o * o
        ms0 = jnp.sum(jnp.where(first_head, o2, 0.0), axis=1, keepdims=True) * (1.0 / HEAD_DIM)
        ms1 = jnp.sum(jnp.where(first_head, 0.0, o2), axis=1, keepdims=True) * (1.0 / HEAD_DIM)
        inv = jnp.where(first_head, lax.rsqrt(ms0 + EPS), lax.rsqrt(ms1 + EPS))
        y = o * inv * g_ref[...] * sg_ref[rows, :]
        o_ref[rows, :] = y.astype(BF16)


def _cumsum_weights():
    j = lax.broadcasted_iota(jnp.int32, (LANES, LANES), 0)
    s = lax.broadcasted_iota(jnp.int32, (LANES, LANES), 1)
    neg_u = jnp.where(j >= s, -1.0, 0.0).astype(BF16)
    half = jnp.concatenate([neg_u, jnp.full((LANES, LANES), -1.0, BF16)], axis=1)
    return jnp.concatenate([half, half], axis=0)


def _head_masks():
    lane = lax.broadcasted_iota(jnp.int32, (2, LANES), 1)
    head = lax.broadcasted_iota(jnp.int32, (2, LANES), 0)
    return ((lane // HEAD_DIM) == head).astype(BF16)


def _attention(q, k, v, sg, g_pair, batch, seq):
    n_pairs = ATTN_DIM // LANES
    seq_spec = pl.BlockSpec((seq, LANES), lambda b, p: (b, p))
    const = lambda b, p: (0, 0)
    return pl.pallas_call(
        _attn_kernel,
        grid=(batch, n_pairs),
        in_specs=[
            seq_spec, seq_spec, seq_spec, seq_spec,
            pl.BlockSpec((1, LANES), const),
            pl.BlockSpec((2, LANES), const),
            pl.BlockSpec((2 * LANES, 2 * LANES), const),
        ],
        out_specs=seq_spec,
        out_shape=jax.ShapeDtypeStruct((batch * seq, ATTN_DIM), BF16),
        scratch_shapes=[
            pltpu.VMEM((2, seq, LANES), BF16),
            pltpu.VMEM((seq // K_CHUNK, 2 * K_CHUNK, LANES), BF16),
            pltpu.VMEM((seq, LANES), F32),
            pltpu.VMEM((2, seq, LANES), F32),
        ],
        compiler_params=pltpu.CompilerParams(
            dimension_semantics=("arbitrary", "arbitrary"), vmem_limit_bytes=VMEM_LIMIT),
        name="sb_attention",
    )(q, k, v, sg, g_pair, _head_masks(), _cumsum_weights())


def _conv_kernel(c_ref, halo_ref, sg_ref, dww_ref, dwb_ref, lng_ref, lnb_ref, wpw_ref, og_ref,
                 o_ref, xw_ref, cv_ref):
    i = pl.program_id(1)
    halo = halo_ref[...]
    xw_ref[0:HALO, :] = jnp.where(i == 0, jnp.zeros_like(halo), halo)
    xw_ref[HALO:, :] = c_ref[...]

    base = HALO - (CONV_WIDTH - 1)
    span = HALO + CONV_ROWS
    for r0 in range(0, CONV_TILE, CONV_ROWS):
        for l0 in range(0, CONV_DIM, LANES):
            acc = jnp.zeros((CONV_ROWS, LANES), F32) + dwb_ref[:, l0:l0 + LANES]
            xs = xw_ref[r0:r0 + span, l0:l0 + LANES]
            for phase in range(SUBLANES):
                taps = [w for w in range(CONV_WIDTH) if (base + w) % SUBLANES == phase]
                if not taps:
                    continue
                y = xs if phase == 0 else pltpu.roll(xs, span - phase, axis=0)
                for w in taps:
                    a = base + w - phase
                    acc = acc + y[a:a + CONV_ROWS, :] * dww_ref[w:w + 1, l0:l0 + LANES]
            cv_ref[r0:r0 + CONV_ROWS, l0:l0 + LANES] = acc

    x = cv_ref[...]
    mu = jnp.mean(x, axis=-1, keepdims=True)
    xc = x - mu
    var = jnp.mean(xc * xc, axis=-1, keepdims=True)
    y = xc * lax.rsqrt(var + EPS) * lng_ref[...] + lnb_ref[...]
    y = y * _sigmoid(y)
    z = jnp.dot(y.astype(BF16), wpw_ref[...], preferred_element_type=F32)
    ms = jnp.mean(z * z, axis=-1, keepdims=True)
    out = z * lax.rsqrt(ms + EPS) * og_ref[...] * sg_ref[...]
    o_ref[...] = out.astype(BF16)


def _conv_branch(c, sg, dw_w, dw_b, ln_g, ln_b, w_pw_bf16, out_g, batch, seq):
    tiles = seq // CONV_TILE
    halo_per_tile = CONV_TILE // HALO
    tile_spec = pl.BlockSpec((CONV_TILE, CONV_DIM), lambda b, i: (b * tiles + i, 0))
    halo_spec = pl.BlockSpec(
        (HALO, CONV_DIM),
        lambda b, i: (jnp.maximum((b * tiles + i) * halo_per_tile - 1, 0), 0))
    const = lambda b, i: (0, 0)
    vec = pl.BlockSpec((1, CONV_DIM), const)
    return pl.pallas_call(
        _conv_kernel,
        grid=(batch, tiles),
        in_specs=[
            tile_spec, halo_spec, tile_spec,
            pl.BlockSpec((CONV_WIDTH, CONV_DIM), const),
            vec, vec, vec,
            pl.BlockSpec((CONV_DIM, CONV_DIM), const),
            vec,
        ],
        out_specs=tile_spec,
        out_shape=jax.ShapeDtypeStruct((batch * seq, CONV_DIM), BF16),
        scratch_shapes=[
            pltpu.VMEM((CONV_TILE + HALO, CONV_DIM), F32),
            pltpu.VMEM((CONV_TILE, CONV_DIM), F32),
        ],
        compiler_params=pltpu.CompilerParams(
            dimension_semantics=("arbitrary", "arbitrary"), vmem_limit_bytes=VMEM_LIMIT),
        name="conv_branch",
    )(c, c, sg, dw_w, dw_b, ln_g, ln_b, w_pw_bf16, out_g)


def _out_proj_kernel(ya_ref, yc_ref, h_ref, p_ref, wa_ref, wc_ref, ng_ref, wg_ref, wp_ref, fg_ref,
                     o_ref, *, final):
    h = h_ref[...]
    h = h + jnp.dot(ya_ref[...], wa_ref[...], preferred_element_type=F32)
    h = h + jnp.dot(yc_ref[...], wc_ref[...], preferred_element_type=F32)
    ms = jnp.mean(h * h, axis=-1, keepdims=True)
    hn = (h * lax.rsqrt(ms + EPS) * ng_ref[...]).astype(BF16)
    gate = _sigmoid(jnp.dot(hn, wg_ref[...], preferred_element_type=F32))
    e = jnp.dot(p_ref[...].astype(BF16), wp_ref[...], preferred_element_type=F32)
    h = h + e * gate
    if final:
        ms = jnp.mean(h * h, axis=-1, keepdims=True)
        h = h * lax.rsqrt(ms + EPS) * fg_ref[...]
    o_ref[...] = h


def _out_proj(ya, yc, h2d, p2d, wa, wc, ng, wg, wp, fg, final):
    m = h2d.shape[0]
    row = lambda i: (i, 0)
    const = lambda i: (0, 0)
    return pl.pallas_call(
        functools.partial(_out_proj_kernel, final=final),
        grid=(m // ROW_TILE,),
        in_specs=[
            pl.BlockSpec((ROW_TILE, ATTN_DIM), row),
            pl.BlockSpec((ROW_TILE, CONV_DIM), row),
            pl.BlockSpec((ROW_TILE, D_MODEL), row),
            pl.BlockSpec((ROW_TILE, PLE_DIM), row),
            pl.BlockSpec((ATTN_DIM, D_MODEL), const),
            pl.BlockSpec((CONV_DIM, D_MODEL), const),
            pl.BlockSpec((1, D_MODEL), const),
            pl.BlockSpec((D_MODEL, D_MODEL), const),
            pl.BlockSpec((PLE_DIM, D_MODEL), const),
            pl.BlockSpec((1, D_MODEL), const),
        ],
        out_specs=pl.BlockSpec((ROW_TILE, D_MODEL), row),
        out_shape=jax.ShapeDtypeStruct((m, D_MODEL), F32),
        compiler_params=pltpu.CompilerParams(
            dimension_semantics=("arbitrary",), vmem_limit_bytes=VMEM_LIMIT),
        name="out_proj_final" if final else "out_proj",
    )(ya, yc, h2d, p2d, wa, wc, ng, wg, wp, fg)


def kernel(x, p, norm_g, w_in, attn_out_g, dw_w, dw_b, conv_ln_g, conv_ln_b, w_pw, conv_out_g,
           w_out, ple_norm_g, w_ple_gate, w_ple, final_g):
    batch, seq, _ = x.shape
    depth = w_in.shape[0]
    m = batch * seq
    h = x.reshape(m, D_MODEL)
    vec = lambda a: a.reshape(1, -1).astype(F32)
    for i in range(depth):
        q, k, v, sga, c, sgc = _in_proj(h, vec(norm_g[i]), w_in[i].astype(BF16))
        g_pair = vec(jnp.concatenate([attn_out_g[i], attn_out_g[i]]))
        ya = _attention(q, k, v, sga, g_pair, batch, seq)
        yc = _conv_branch(c, sgc, dw_w[i].astype(F32), vec(dw_b[i]), vec(conv_ln_g[i]),
                          vec(conv_ln_b[i]), w_pw[i].astype(BF16), vec(conv_out_g[i]), batch, seq)
        w_o = w_out[i].astype(BF16)
        h = _out_proj(ya, yc, h, p[i].reshape(m, PLE_DIM), w_o[:ATTN_DIM], w_o[ATTN_DIM:],
                      vec(ple_norm_g[i]), w_ple_gate[i].astype(BF16), w_ple[i].astype(BF16),
                      vec(final_g), final=(i == depth - 1))
    return h.reshape(batch, seq, D_MODEL)
```

```python
import functools

import jax
import jax.numpy as jnp
from jax import lax
from jax.experimental import pallas as pl
from jax.experimental.pallas import tpu as pltpu

D_MODEL = 1024
ATTN_DIM = 512
CONV_DIM = 512
HEAD_DIM = 64
CONV_WIDTH = 31
PLE_DIM = 256
D_IN = 4 * ATTN_DIM + 3 * CONV_DIM
EPS = 1e-6

LANES = 128
SUBLANES = 8
K_CHUNK = 256
HEADS_IN_FLIGHT = 1
HALO = 32
ROW_TILE = 512
OUT_TILE = 1024
CONV_TILE = 512
CONV_ROWS = 128
VMEM_LIMIT = 56 * 1024 * 1024
LOG2E = 1.4426950408889634
EXP_UNDERFLOW = -104.0

F32 = jnp.float32
BF16 = jnp.bfloat16


def _sigmoid(x):
    return 1.0 / (1.0 + jnp.exp(-x))


def _in_proj_kernel(x_ref, g_ref, w_ref, q_ref, k_ref, v_ref, sga_ref, c_ref, sgc_ref):
    x = x_ref[...]
    ms = jnp.mean(x * x, axis=-1, keepdims=True)
    hn = (x * lax.rsqrt(ms + EPS) * g_ref[...]).astype(BF16)

    def seg(i):
        return jnp.dot(hn, w_ref[:, i * ATTN_DIM:(i + 1) * ATTN_DIM], preferred_element_type=F32)

    q_ref[...] = (seg(0) * (HEAD_DIM ** -0.5)).astype(BF16)
    k_ref[...] = seg(1).astype(BF16)
    v_ref[...] = seg(2).astype(BF16)
    ga = seg(3)
    sga_ref[...] = ga * _sigmoid(ga)
    c_ref[...] = seg(4) * _sigmoid(seg(5))
    gc = seg(6)
    sgc_ref[...] = gc * _sigmoid(gc)


def _in_proj(h2d, g, w_bf16):
    m = h2d.shape[0]
    row = lambda i: (i, 0)
    const = lambda i: (0, 0)
    seg_spec = pl.BlockSpec((ROW_TILE, ATTN_DIM), row)
    return pl.pallas_call(
        _in_proj_kernel,
        grid=(m // ROW_TILE,),
        in_specs=[
            pl.BlockSpec((ROW_TILE, D_MODEL), row),
            pl.BlockSpec((1, D_MODEL), const),
            pl.BlockSpec((D_MODEL, D_IN), const),
        ],
        out_specs=[seg_spec] * 6,
        out_shape=[
            jax.ShapeDtypeStruct((m, ATTN_DIM), BF16),
            jax.ShapeDtypeStruct((m, ATTN_DIM), BF16),
            jax.ShapeDtypeStruct((m, ATTN_DIM), BF16),
            jax.ShapeDtypeStruct((m, ATTN_DIM), F32),
            jax.ShapeDtypeStruct((m, CONV_DIM), F32),
            jax.ShapeDtypeStruct((m, CONV_DIM), F32),
        ],
        compiler_params=pltpu.CompilerParams(
            dimension_semantics=("arbitrary",), vmem_limit_bytes=VMEM_LIMIT),
        name="in_proj",
    )(h2d, g, w_bf16)


def _attn_kernel(q_ref, k_ref, v_ref, sg_ref, g_ref, hm_ref, w_ref, o_ref,
                 qm_ref, vm_ref, acc_ref, carry_ref):
    seq = q_ref.shape[0]
    n_chunks = seq // K_CHUNK
    n_sub = K_CHUNK // LANES
    row = lax.broadcasted_iota(jnp.int32, (K_CHUNK, K_CHUNK), 0)
    col = lax.broadcasted_iota(jnp.int32, (K_CHUNK, K_CHUNK), 1)
    causal = col < row

    for h in range(2):
        qm_ref[h] = q_ref[...] * hm_ref[h:h + 1, :]
        for j in range(n_chunks):
            vm_ref[j, h * K_CHUNK:(h + 1) * K_CHUNK, :] = (
                v_ref[j * K_CHUNK:(j + 1) * K_CHUNK, :] * hm_ref[h:h + 1, :])

    def on_rows(m, has_diag, diag_fn, rest_fn):
        n_diag = K_CHUNK if has_diag else 0
        parts = [diag_fn(slice(0, n_diag))] if has_diag else []
        if m > n_diag:
            parts.append(rest_fn(slice(n_diag, m)))
        return parts[0] if len(parts) == 1 else jnp.concatenate(parts, axis=0)

    def head(q0, m, j, has_diag):
        kc = k_ref[pl.ds(j * K_CHUNK, K_CHUNK), :]
        zs, lhs = [], []
        for h in range(2):
            z = lax.dot_general(qm_ref[h, pl.ds(q0, m), :], kc, (((1,), (1,)), ((), ())),
                                preferred_element_type=F32)
            sp = jnp.maximum(z, 0.0) + jnp.log(1.0 + jnp.exp2(jnp.abs(z) * -LOG2E))
            sp = on_rows(m, has_diag, lambda d: jnp.where(causal, sp[d], 0.0), lambda b: sp[b])
            hi = sp.astype(BF16)
            lo = (sp - hi.astype(F32)).astype(BF16)
            zs.append(z)
            for s in reversed(range(n_sub)):
                sl = slice(s * LANES, (s + 1) * LANES)
                lhs.append(jnp.concatenate([hi[:, sl], lo[:, sl]], axis=1))
        r = jnp.dot(jnp.concatenate(lhs, axis=0), w_ref[...], preferred_element_type=F32)
        return q0, m, j, has_diag, zs, r

    def tail(q0, m, j, has_diag, zs, r):
        n_diag = K_CHUNK if has_diag else 0
        ps = []
        for h in range(2):
            z = zs[h]
            c = None
            p_sub = [None] * n_sub
            for idx, s in enumerate(reversed(range(n_sub))):
                sl = slice(s * LANES, (s + 1) * LANES)
                rr = r[(h * n_sub + idx) * m:(h * n_sub + idx + 1) * m]
                arg = z[:, sl] + rr[:, :LANES]
                if c is None:
                    c_in = carry_ref[h, pl.ds(q0 + n_diag, m - n_diag), :] if m > n_diag else None
                    arg = on_rows(m, has_diag, lambda d: arg[d], lambda b: arg[b] + c_in)
                    c = on_rows(m, has_diag, lambda d: rr[d, LANES:],
                                lambda b: c_in + rr[b, LANES:])
                else:
                    arg = arg + c
                    c = c + rr[:, LANES:]
                p_sub[s] = jnp.exp(arg)
            carry_ref[h, pl.ds(q0, m), :] = c
            p = jnp.concatenate(p_sub, axis=1)
            p = on_rows(m, has_diag, lambda d: jnp.where(causal, p[d], 0.0), lambda b: p[b])
            ps.append(p.astype(BF16))
        pv = jnp.dot(jnp.concatenate(ps, axis=1), vm_ref[j], preferred_element_type=F32)
        if has_diag:
            acc_ref[pl.ds(q0, n_diag), :] = pv[:n_diag]
        if m > n_diag:
            acc_ref[pl.ds(q0 + n_diag, m - n_diag), :] += pv[n_diag:]

    pending = []
    for j in reversed(range(n_chunks)):
        pending.append(head(j * K_CHUNK, min(2 * K_CHUNK, seq - j * K_CHUNK), j, True))
        if len(pending) > HEADS_IN_FLIGHT:
            tail(*pending.pop(0))
    for issued in pending:
        tail(*issued)

    def finish_block(i, _):
        q0 = pl.multiple_of(i * K_CHUNK, K_CHUNK)

        def alive():
            return jnp.max(carry_ref[:, pl.ds(q0, K_CHUNK), :]) > EXP_UNDERFLOW

        def more(state):
            j, live = state
            return jnp.logical_and(j >= 0, live)

        def step(state):
            j, _ = state
            tail(*head(q0, K_CHUNK, j, False))
            return j - 1, alive()

        lax.while_loop(more, step, (i - 2, alive()))
        return 0

    lax.fori_loop(2, n_chunks, finish_block, 0)

    lane = lax.broadcasted_iota(jnp.int32, (K_CHUNK, LANES), 1)
    first_head = lane < HEAD_DIM
    for i in range(n_chunks):
        rows = slice(i * K_CHUNK, (i + 1) * K_CHUNK)
        o = acc_ref[rows, :]
        o2 = o * o
        ms0 = jnp.sum(jnp.where(first_head, o2, 0.0), axis=1, keepdims=True) * (1.0 / HEAD_DIM)
        ms1 = jnp.sum(jnp.where(first_head, 0.0, o2), axis=1, keepdims=True) * (1.0 / HEAD_DIM)
        inv = jnp.where(first_head, lax.rsqrt(ms0 + EPS), lax.rsqrt(ms1 + EPS))
        y = o * inv * g_ref[...] * sg_ref[rows, :]
        o_ref[rows, :] = y.astype(BF16)


def _cumsum_weights():
    j = lax.broadcasted_iota(jnp.int32, (LANES, LANES), 0)
    s = lax.broadcasted_iota(jnp.int32, (LANES, LANES), 1)
    neg_u = jnp.where(j >= s, -1.0, 0.0).astype(BF16)
    half = jnp.concatenate([neg_u, jnp.full((LANES, LANES), -1.0, BF16)], axis=1)
    return jnp.concatenate([half, half], axis=0)


def _head_masks():
    lane = lax.broadcasted_iota(jnp.int32, (2, LANES), 1)
    head = lax.broadcasted_iota(jnp.int32, (2, LANES), 0)
    return ((lane // HEAD_DIM) == head).astype(BF16)


def _attention(q, k, v, sg, g_pair, batch, seq):
    n_pairs = ATTN_DIM // LANES
    seq_spec = pl.BlockSpec((seq, LANES), lambda b, p: (b, p))
    const = lambda b, p: (0, 0)
    return pl.pallas_call(
        _attn_kernel,
        grid=(batch, n_pairs),
        in_specs=[
            seq_spec, seq_spec, seq_spec, seq_spec,
            pl.BlockSpec((1, LANES), const),
            pl.BlockSpec((2, LANES), const),
            pl.BlockSpec((2 * LANES, 2 * LANES), const),
        ],
        out_specs=seq_spec,
        out_shape=jax.ShapeDtypeStruct((batch * seq, ATTN_DIM), BF16),
        scratch_shapes=[
            pltpu.VMEM((2, seq, LANES), BF16),
            pltpu.VMEM((seq // K_CHUNK, 2 * K_CHUNK, LANES), BF16),
            pltpu.VMEM((seq, LANES), F32),
            pltpu.VMEM((2, seq, LANES), F32),
        ],
        compiler_params=pltpu.CompilerParams(
            dimension_semantics=("arbitrary", "arbitrary"), vmem_limit_bytes=VMEM_LIMIT),
        name="sb_attention",
    )(q, k, v, sg, g_pair, _head_masks(), _cumsum_weights())


def _conv_kernel(c_ref, halo_ref, sg_ref, dww_ref, dwb_ref, lng_ref, lnb_ref, wpw_ref, og_ref,
                 o_ref, xw_ref, cv_ref):
    i = pl.program_id(1)
    halo = halo_ref[...]
    xw_ref[0:HALO, :] = jnp.where(i == 0, jnp.zeros_like(halo), halo)
    xw_ref[HALO:, :] = c_ref[...]

    base = HALO - (CONV_WIDTH - 1)
    span = HALO + CONV_ROWS
    for r0 in range(0, CONV_TILE, CONV_ROWS):
        for l0 in range(0, CONV_DIM, LANES):
            acc = jnp.zeros((CONV_ROWS, LANES), F32) + dwb_ref[:, l0:l0 + LANES]
            xs = xw_ref[r0:r0 + span, l0:l0 + LANES]
            for phase in range(SUBLANES):
                taps = [w for w in range(CONV_WIDTH) if (base + w) % SUBLANES == phase]
                if not taps:
                    continue
                y = xs if phase == 0 else pltpu.roll(xs, span - phase, axis=0)
                for w in taps:
                    a = base + w - phase
                    acc = acc + y[a:a + CONV_ROWS, :] * dww_ref[w:w + 1, l0:l0 + LANES]
            cv_ref[r0:r0 + CONV_ROWS, l0:l0 + LANES] = acc

    x = cv_ref[...]
    mu = jnp.mean(x, axis=-1, keepdims=True)
    xc = x - mu
    var = jnp.mean(xc * xc, axis=-1, keepdims=True)
    y = xc * lax.rsqrt(var + EPS) * lng_ref[...] + lnb_ref[...]
    y = y * _sigmoid(y)
    z = jnp.dot(y.astype(BF16), wpw_ref[...], preferred_element_type=F32)
    ms = jnp.mean(z * z, axis=-1, keepdims=True)
    out = z * lax.rsqrt(ms + EPS) * og_ref[...] * sg_ref[...]
    o_ref[...] = out.astype(BF16)


def _conv_branch(c, sg, dw_w, dw_b, ln_g, ln_b, w_pw_bf16, out_g, batch, seq):
    tiles = seq // CONV_TILE
    halo_per_tile = CONV_TILE // HALO
    tile_spec = pl.BlockSpec((CONV_TILE, CONV_DIM), lambda b, i: (b * tiles + i, 0))
    halo_spec = pl.BlockSpec(
        (HALO, CONV_DIM),
        lambda b, i: (jnp.maximum((b * tiles + i) * halo_per_tile - 1, 0), 0))
    const = lambda b, i: (0, 0)
    vec = pl.BlockSpec((1, CONV_DIM), const)
    return pl.pallas_call(
        _conv_kernel,
        grid=(batch, tiles),
        in_specs=[
            tile_spec, halo_spec, tile_spec,
            pl.BlockSpec((CONV_WIDTH, CONV_DIM), const),
            vec, vec, vec,
            pl.BlockSpec((CONV_DIM, CONV_DIM), const),
            vec,
        ],
        out_specs=tile_spec,
        out_shape=jax.ShapeDtypeStruct((batch * seq, CONV_DIM), BF16),
        scratch_shapes=[
            pltpu.VMEM((CONV_TILE + HALO, CONV_DIM), F32),
            pltpu.VMEM((CONV_TILE, CONV_DIM), F32),
        ],
        compiler_params=pltpu.CompilerParams(
            dimension_semantics=("arbitrary", "arbitrary"), vmem_limit_bytes=VMEM_LIMIT),
        name="conv_branch",
    )(c, c, sg, dw_w, dw_b, ln_g, ln_b, w_pw_bf16, out_g)


def _out_proj_kernel(ya_ref, yc_ref, h_ref, p_ref, wa_ref, wc_ref, ng_ref, wg_ref, wp_ref, fg_ref,
                     o_ref, *, final):
    def mix(rows):
        h = h_ref[rows, :]
        h = h + jnp.dot(ya_ref[rows, :], wa_ref[...], preferred_element_type=F32)
        h = h + jnp.dot(yc_ref[rows, :], wc_ref[...], preferred_element_type=F32)
        ms = jnp.mean(h * h, axis=-1, keepdims=True)
        return h, (h * lax.rsqrt(ms + EPS) * ng_ref[...]).astype(BF16)

    def embed(rows, h, hn):
        gate = _sigmoid(jnp.dot(hn, wg_ref[...], preferred_element_type=F32))
        e = jnp.dot(p_ref[rows, :].astype(BF16), wp_ref[...], preferred_element_type=F32)
        h = h + e * gate
        if final:
            ms = jnp.mean(h * h, axis=-1, keepdims=True)
            h = h * lax.rsqrt(ms + EPS) * fg_ref[...]
        o_ref[rows, :] = h

    n_rows = h_ref.shape[0]
    halves = [slice(0, n_rows // 2), slice(n_rows // 2, n_rows)]
    mixed = [mix(rows) for rows in halves]
    for rows, (h, hn) in zip(halves, mixed):
        embed(rows, h, hn)


def _out_proj(ya, yc, h2d, p2d, wa, wc, ng, wg, wp, fg, final):
    m = h2d.shape[0]
    row = lambda i: (i, 0)
    const = lambda i: (0, 0)
    return pl.pallas_call(
        functools.partial(_out_proj_kernel, final=final),
        grid=(m // OUT_TILE,),
        in_specs=[
            pl.BlockSpec((OUT_TILE, ATTN_DIM), row),
            pl.BlockSpec((OUT_TILE, CONV_DIM), row),
            pl.BlockSpec((OUT_TILE, D_MODEL), row),
            pl.BlockSpec((OUT_TILE, PLE_DIM), row),
            pl.BlockSpec((ATTN_DIM, D_MODEL), const),
            pl.BlockSpec((CONV_DIM, D_MODEL), const),
            pl.BlockSpec((1, D_MODEL), const),
            pl.BlockSpec((D_MODEL, D_MODEL), const),
            pl.BlockSpec((PLE_DIM, D_MODEL), const),
            pl.BlockSpec((1, D_MODEL), const),
        ],
        out_specs=pl.BlockSpec((OUT_TILE, D_MODEL), row),
        out_shape=jax.ShapeDtypeStruct((m, D_MODEL), F32),
        compiler_params=pltpu.CompilerParams(
            dimension_semantics=("arbitrary",), vmem_limit_bytes=VMEM_LIMIT),
        name="out_proj_final" if final else "out_proj",
    )(ya, yc, h2d, p2d, wa, wc, ng, wg, wp, fg)


def kernel(x, p, norm_g, w_in, attn_out_g, dw_w, dw_b, conv_ln_g, conv_ln_b, w_pw, conv_out_g,
           w_out, ple_norm_g, w_ple_gate, w_ple, final_g):
    batch, seq, _ = x.shape
    depth = w_in.shape[0]
    m = batch * seq
    h = x.reshape(m, D_MODEL)
    vec = lambda a: a.reshape(1, -1).astype(F32)
    for i in range(depth):
        q, k, v, sga, c, sgc = _in_proj(h, vec(norm_g[i]), w_in[i].astype(BF16))
        g_pair = vec(jnp.concatenate([attn_out_g[i], attn_out_g[i]]))
        ya = _attention(q, k, v, sga, g_pair, batch, seq)
        yc = _conv_branch(c, sgc, dw_w[i].astype(F32), vec(dw_b[i]), vec(conv_ln_g[i]),
                          vec(conv_ln_b[i]), w_pw[i].astype(BF16), vec(conv_out_g[i]), batch, seq)
        w_o = w_out[i].astype(BF16)
        h = _out_proj(ya, yc, h, p[i].reshape(m, PLE_DIM), w_o[:ATTN_DIM], w_o[ATTN_DIM:],
                      vec(ple_norm_g[i]), w_ple_gate[i].astype(BF16), w_ple[i].astype(BF16),
                      vec(final_g), final=(i == depth - 1))
    return h.reshape(batch, seq, D_MODEL)
```

```python
import functools

import jax
import jax.numpy as jnp
from jax import lax
from jax.experimental import pallas as pl
from jax.experimental.pallas import tpu as pltpu

D_MODEL = 1024
ATTN_DIM = 512
CONV_DIM = 512
HEAD_DIM = 64
CONV_WIDTH = 31
PLE_DIM = 256
D_IN = 4 * ATTN_DIM + 3 * CONV_DIM
EPS = 1e-6

LANES = 128
SUBLANES = 8
K_CHUNK = 256
HEADS_IN_FLIGHT = 1
HALO = 32
ROW_TILE = 512
OUT_TILE = 1024
CONV_TILE = 512
CONV_ROWS = 128
VMEM_LIMIT = 56 * 1024 * 1024
LOG2E = 1.4426950408889634
EXP_UNDERFLOW = -104.0

F32 = jnp.float32
BF16 = jnp.bfloat16


def _sigmoid(x):
    return 1.0 / (1.0 + jnp.exp(-x))


def _cast_once(pairs, grid_rank=1):
    first = pl.program_id(0) == 0
    for axis in range(1, grid_rank):
        first = jnp.logical_and(first, pl.program_id(axis) == 0)

    @pl.when(first)
    def _():
        for src_ref, dst_ref in pairs:
            for c0 in range(0, src_ref.shape[1], ATTN_DIM):
                dst_ref[:, c0:c0 + ATTN_DIM] = src_ref[:, c0:c0 + ATTN_DIM].astype(BF16)


def _in_proj_kernel(x_ref, g_ref, w32_ref, q_ref, k_ref, v_ref, sga_ref, c_ref, sgc_ref, w_ref):
    _cast_once([(w32_ref, w_ref)])
    x = x_ref[...]
    ms = jnp.mean(x * x, axis=-1, keepdims=True)
    hn = (x * lax.rsqrt(ms + EPS) * g_ref[...]).astype(BF16)

    def seg(i):
        return jnp.dot(hn, w_ref[:, i * ATTN_DIM:(i + 1) * ATTN_DIM], preferred_element_type=F32)

    q_ref[...] = (seg(0) * (HEAD_DIM ** -0.5)).astype(BF16)
    k_ref[...] = seg(1).astype(BF16)
    v_ref[...] = seg(2).astype(BF16)
    ga = seg(3)
    sga_ref[...] = ga * _sigmoid(ga)
    c_ref[...] = seg(4) * _sigmoid(seg(5))
    gc = seg(6)
    sgc_ref[...] = gc * _sigmoid(gc)


def _layer_spec(layer, shape):
    zeros = (0,) * len(shape)
    return pl.BlockSpec((None,) + tuple(shape), lambda *_: (layer,) + zeros,
                        pipeline_mode=pl.Buffered(1))


def _in_proj(h2d, norm_g, w_in, layer):
    m = h2d.shape[0]
    row = lambda i: (i, 0)
    seg_spec = pl.BlockSpec((ROW_TILE, ATTN_DIM), row)
    return pl.pallas_call(
        _in_proj_kernel,
        grid=(m // ROW_TILE,),
        in_specs=[
            pl.BlockSpec((ROW_TILE, D_MODEL), row),
            _layer_spec(layer, (1, D_MODEL)),
            _layer_spec(layer, (D_MODEL, D_IN)),
        ],
        out_specs=[seg_spec] * 6,
        out_shape=[
            jax.ShapeDtypeStruct((m, ATTN_DIM), BF16),
            jax.ShapeDtypeStruct((m, ATTN_DIM), BF16),
            jax.ShapeDtypeStruct((m, ATTN_DIM), BF16),
            jax.ShapeDtypeStruct((m, ATTN_DIM), F32),
            jax.ShapeDtypeStruct((m, CONV_DIM), F32),
            jax.ShapeDtypeStruct((m, CONV_DIM), F32),
        ],
        scratch_shapes=[pltpu.VMEM((D_MODEL, D_IN), BF16)],
        compiler_params=pltpu.CompilerParams(
            dimension_semantics=("arbitrary",), vmem_limit_bytes=VMEM_LIMIT),
        name="in_proj",
    )(h2d, norm_g, w_in)


def _attn_kernel(q_ref, k_ref, v_ref, sg_ref, g_ref, hm_ref, w_ref, o_ref,
                 qm_ref, vm_ref, acc_ref, carry_ref):
    seq = q_ref.shape[0]
    n_chunks = seq // K_CHUNK
    n_sub = K_CHUNK // LANES
    row = lax.broadcasted_iota(jnp.int32, (K_CHUNK, K_CHUNK), 0)
    col = lax.broadcasted_iota(jnp.int32, (K_CHUNK, K_CHUNK), 1)
    causal = col < row

    for h in range(2):
        qm_ref[h] = q_ref[...] * hm_ref[h:h + 1, :]
        for j in range(n_chunks):
            vm_ref[j, h * K_CHUNK:(h + 1) * K_CHUNK, :] = (
                v_ref[j * K_CHUNK:(j + 1) * K_CHUNK, :] * hm_ref[h:h + 1, :])

    def on_rows(m, has_diag, diag_fn, rest_fn):
        n_diag = K_CHUNK if has_diag else 0
        parts = [diag_fn(slice(0, n_diag))] if has_diag else []
        if m > n_diag:
            parts.append(rest_fn(slice(n_diag, m)))
        return parts[0] if len(parts) == 1 else jnp.concatenate(parts, axis=0)

    def head(q0, m, j, has_diag):
        kc = k_ref[pl.ds(j * K_CHUNK, K_CHUNK), :]
        zs, lhs = [], []
        for h in range(2):
            z = lax.dot_general(qm_ref[h, pl.ds(q0, m), :], kc, (((1,), (1,)), ((), ())),
                                preferred_element_type=F32)
            sp = jnp.maximum(z, 0.0) + jnp.log(1.0 + jnp.exp2(jnp.abs(z) * -LOG2E))
            sp = on_rows(m, has_diag, lambda d: jnp.where(causal, sp[d], 0.0), lambda b: sp[b])
            hi = sp.astype(BF16)
            lo = (sp - hi.astype(F32)).astype(BF16)
            zs.append(z)
            for s in reversed(range(n_sub)):
                sl = slice(s * LANES, (s + 1) * LANES)
                lhs.append(jnp.concatenate([hi[:, sl], lo[:, sl]], axis=1))
        r = jnp.dot(jnp.concatenate(lhs, axis=0), w_ref[...], preferred_element_type=F32)
        return q0, m, j, has_diag, zs, r

    def tail(q0, m, j, has_diag, zs, r):
        n_diag = K_CHUNK if has_diag else 0
        ps = []
        for h in range(2):
            z = zs[h]
            c = None
            p_sub = [None] * n_sub
            for idx, s in enumerate(reversed(range(n_sub))):
                sl = slice(s * LANES, (s + 1) * LANES)
                rr = r[(h * n_sub + idx) * m:(h * n_sub + idx + 1) * m]
                arg = z[:, sl] + rr[:, :LANES]
                if c is None:
                    c_in = carry_ref[h, pl.ds(q0 + n_diag, m - n_diag), :] if m > n_diag else None
                    arg = on_rows(m, has_diag, lambda d: arg[d], lambda b: arg[b] + c_in)
                    c = on_rows(m, has_diag, lambda d: rr[d, LANES:],
                                lambda b: c_in + rr[b, LANES:])
                else:
                    arg = arg + c
                    c = c + rr[:, LANES:]
                p_sub[s] = jnp.exp(arg)
            carry_ref[h, pl.ds(q0, m), :] = c
            p = jnp.concatenate(p_sub, axis=1)
            p = on_rows(m, has_diag, lambda d: jnp.where(causal, p[d], 0.0), lambda b: p[b])
            ps.append(p.astype(BF16))
        pv = jnp.dot(jnp.concatenate(ps, axis=1), vm_ref[j], preferred_element_type=F32)
        if has_diag:
            acc_ref[pl.ds(q0, n_diag), :] = pv[:n_diag]
        if m > n_diag:
            acc_ref[pl.ds(q0 + n_diag, m - n_diag), :] += pv[n_diag:]

    pending = []
    for j in reversed(range(n_chunks)):
        pending.append(head(j * K_CHUNK, min(2 * K_CHUNK, seq - j * K_CHUNK), j, True))
        if len(pending) > HEADS_IN_FLIGHT:
            tail(*pending.pop(0))
    for issued in pending:
        tail(*issued)

    def finish_block(i, _):
        q0 = pl.multiple_of(i * K_CHUNK, K_CHUNK)

        def alive():
            return jnp.max(carry_ref[:, pl.ds(q0, K_CHUNK), :]) > EXP_UNDERFLOW

        def more(state):
            j, live = state
            return jnp.logical_and(j >= 0, live)

        def step(state):
            j, _ = state
            tail(*head(q0, K_CHUNK, j, False))
            return j - 1, alive()

        lax.while_loop(more, step, (i - 2, alive()))
        return 0

    lax.fori_loop(2, n_chunks, finish_block, 0)

    lane = lax.broadcasted_iota(jnp.int32, (K_CHUNK, LANES), 1)
    first_head = lane < HEAD_DIM
    for i in range(n_chunks):
        rows = slice(i * K_CHUNK, (i + 1) * K_CHUNK)
        o = acc_ref[rows, :]
        o2 = o * o
        ms0 = jnp.sum(jnp.where(first_head, o2, 0.0), axis=1, keepdims=True) * (1.0 / HEAD_DIM)
        ms1 = jnp.sum(jnp.where(first_head, 0.0, o2), axis=1, keepdims=True) * (1.0 / HEAD_DIM)
        inv = jnp.where(first_head, lax.rsqrt(ms0 + EPS), lax.rsqrt(ms1 + EPS))
        y = o * inv * g_ref[...] * sg_ref[rows, :]
        o_ref[rows, :] = y.astype(BF16)


def _cumsum_weights():
    j = lax.broadcasted_iota(jnp.int32, (LANES, LANES), 0)
    s = lax.broadcasted_iota(jnp.int32, (LANES, LANES), 1)
    neg_u = jnp.where(j >= s, -1.0, 0.0).astype(BF16)
    half = jnp.concatenate([neg_u, jnp.full((LANES, LANES), -1.0, BF16)], axis=1)
    return jnp.concatenate([half, half], axis=0)


def _head_masks():
    lane = lax.broadcasted_iota(jnp.int32, (2, LANES), 1)
    head = lax.broadcasted_iota(jnp.int32, (2, LANES), 0)
    return ((lane // HEAD_DIM) == head).astype(BF16)


def _attention(q, k, v, sg, g_pair, layer, batch, seq):
    n_pairs = ATTN_DIM // LANES
    seq_spec = pl.BlockSpec((seq, LANES), lambda b, p: (b, p))
    const = lambda b, p: (0, 0)
    return pl.pallas_call(
        _attn_kernel,
        grid=(batch, n_pairs),
        in_specs=[
            seq_spec, seq_spec, seq_spec, seq_spec,
            _layer_spec(layer, (1, LANES)),
            pl.BlockSpec((2, LANES), const),
            pl.BlockSpec((2 * LANES, 2 * LANES), const),
        ],
        out_specs=seq_spec,
        out_shape=jax.ShapeDtypeStruct((batch * seq, ATTN_DIM), BF16),
        scratch_shapes=[
            pltpu.VMEM((2, seq, LANES), BF16),
            pltpu.VMEM((seq // K_CHUNK, 2 * K_CHUNK, LANES), BF16),
            pltpu.VMEM((seq, LANES), F32),
            pltpu.VMEM((2, seq, LANES), F32),
        ],
        compiler_params=pltpu.CompilerParams(
            dimension_semantics=("arbitrary", "arbitrary"), vmem_limit_bytes=VMEM_LIMIT),
        name="sb_attention",
    )(q, k, v, sg, g_pair, _head_masks(), _cumsum_weights())


def _conv_kernel(c_ref, halo_ref, sg_ref, dww_ref, dwb_ref, lng_ref, lnb_ref, wpw32_ref, og_ref,
                 o_ref, xw_ref, cv_ref, wpw_ref):
    _cast_once([(wpw32_ref, wpw_ref)], grid_rank=2)
    i = pl.program_id(1)
    halo = halo_ref[...]
    xw_ref[0:HALO, :] = jnp.where(i == 0, jnp.zeros_like(halo), halo)
    xw_ref[HALO:, :] = c_ref[...]

    base = HALO - (CONV_WIDTH - 1)
    span = HALO + CONV_ROWS
    for r0 in range(0, CONV_TILE, CONV_ROWS):
        for l0 in range(0, CONV_DIM, LANES):
            acc = jnp.zeros((CONV_ROWS, LANES), F32) + dwb_ref[:, l0:l0 + LANES]
            xs = xw_ref[r0:r0 + span, l0:l0 + LANES]
            for phase in range(SUBLANES):
                taps = [w for w in range(CONV_WIDTH) if (base + w) % SUBLANES == phase]
                if not taps:
                    continue
                y = xs if phase == 0 else pltpu.roll(xs, span - phase, axis=0)
                for w in taps:
                    a = base + w - phase
                    acc = acc + y[a:a + CONV_ROWS, :] * dww_ref[w:w + 1, l0:l0 + LANES]
            cv_ref[r0:r0 + CONV_ROWS, l0:l0 + LANES] = acc

    x = cv_ref[...]
    mu = jnp.mean(x, axis=-1, keepdims=True)
    xc = x - mu
    var = jnp.mean(xc * xc, axis=-1, keepdims=True)
    y = xc * lax.rsqrt(var + EPS) * lng_ref[...] + lnb_ref[...]
    y = y * _sigmoid(y)
    z = jnp.dot(y.astype(BF16), wpw_ref[...], preferred_element_type=F32)
    ms = jnp.mean(z * z, axis=-1, keepdims=True)
    out = z * lax.rsqrt(ms + EPS) * og_ref[...] * sg_ref[...]
    o_ref[...] = out.astype(BF16)


def _conv_branch(c, sg, dw_w, dw_b, ln_g, ln_b, w_pw, out_g, layer, batch, seq):
    tiles = seq // CONV_TILE
    halo_per_tile = CONV_TILE // HALO
    tile_spec = pl.BlockSpec((CONV_TILE, CONV_DIM), lambda b, i: (b * tiles + i, 0))
    halo_spec = pl.BlockSpec(
        (HALO, CONV_DIM),
        lambda b, i: (jnp.maximum((b * tiles + i) * halo_per_tile - 1, 0), 0))
    vec = _layer_spec(layer, (1, CONV_DIM))
    return pl.pallas_call(
        _conv_kernel,
        grid=(batch, tiles),
        in_specs=[
            tile_spec, halo_spec, tile_spec,
            _layer_spec(layer, (CONV_WIDTH, CONV_DIM)),
            vec, vec, vec,
            _layer_spec(layer, (CONV_DIM, CONV_DIM)),
            vec,
        ],
        out_specs=tile_spec,
        out_shape=jax.ShapeDtypeStruct((batch * seq, CONV_DIM), BF16),
        scratch_shapes=[
            pltpu.VMEM((CONV_TILE + HALO, CONV_DIM), F32),
            pltpu.VMEM((CONV_TILE, CONV_DIM), F32),
            pltpu.VMEM((CONV_DIM, CONV_DIM), BF16),
        ],
        compiler_params=pltpu.CompilerParams(
            dimension_semantics=("arbitrary", "arbitrary"), vmem_limit_bytes=VMEM_LIMIT),
        name="conv_branch",
    )(c, c, sg, dw_w, dw_b, ln_g, ln_b, w_pw, out_g)


def _out_proj_kernel(ya_ref, yc_ref, h_ref, p_ref, wo32_ref, ng_ref, wg32_ref, wp32_ref, fg_ref,
                     o_ref, wo_ref, wg_ref, wp_ref, *, final):
    _cast_once([(wo32_ref, wo_ref), (wg32_ref, wg_ref), (wp32_ref, wp_ref)])

    def mix(rows):
        h = h_ref[rows, :]
        h = h + jnp.dot(ya_ref[rows, :], wo_ref[0:ATTN_DIM, :], preferred_element_type=F32)
        h = h + jnp.dot(yc_ref[rows, :], wo_ref[ATTN_DIM:, :], preferred_element_type=F32)
        ms = jnp.mean(h * h, axis=-1, keepdims=True)
        return h, (h * lax.rsqrt(ms + EPS) * ng_ref[...]).astype(BF16)

    def embed(rows, h, hn):
        gate = _sigmoid(jnp.dot(hn, wg_ref[...], preferred_element_type=F32))
        e = jnp.dot(p_ref[rows, :].astype(BF16), wp_ref[...], preferred_element_type=F32)
        h = h + e * gate
        if final:
            ms = jnp.mean(h * h, axis=-1, keepdims=True)
            h = h * lax.rsqrt(ms + EPS) * fg_ref[...]
        o_ref[rows, :] = h

    n_rows = h_ref.shape[0]
    halves = [slice(0, n_rows // 2), slice(n_rows // 2, n_rows)]
    mixed = [mix(rows) for rows in halves]
    for rows, (h, hn) in zip(halves, mixed):
        embed(rows, h, hn)


def _out_proj(ya, yc, h2d, p3d, w_out, ple_norm_g, w_ple_gate, w_ple, final_g, layer, final):
    m = h2d.shape[0]
    row = lambda i: (i, 0)
    return pl.pallas_call(
        functools.partial(_out_proj_kernel, final=final),
        grid=(m // OUT_TILE,),
        in_specs=[
            pl.BlockSpec((OUT_TILE, ATTN_DIM), row),
            pl.BlockSpec((OUT_TILE, CONV_DIM), row),
            pl.BlockSpec((OUT_TILE, D_MODEL), row),
            pl.BlockSpec((None, OUT_TILE, PLE_DIM), lambda i: (layer, i, 0)),
            _layer_spec(layer, (D_MODEL, D_MODEL)),
            _layer_spec(layer, (1, D_MODEL)),
            _layer_spec(layer, (D_MODEL, D_MODEL)),
            _layer_spec(layer, (PLE_DIM, D_MODEL)),
            pl.BlockSpec((1, D_MODEL), lambda i: (0, 0)),
        ],
        out_specs=pl.BlockSpec((OUT_TILE, D_MODEL), row),
        out_shape=jax.ShapeDtypeStruct((m, D_MODEL), F32),
        scratch_shapes=[
            pltpu.VMEM((D_MODEL, D_MODEL), BF16),
            pltpu.VMEM((D_MODEL, D_MODEL), BF16),
            pltpu.VMEM((PLE_DIM, D_MODEL), BF16),
        ],
        compiler_params=pltpu.CompilerParams(
            dimension_semantics=("arbitrary",), vmem_limit_bytes=VMEM_LIMIT),
        name="out_proj_final" if final else "out_proj",
    )(ya, yc, h2d, p3d, w_out, ple_norm_g, w_ple_gate, w_ple, final_g)


def kernel(x, p, norm_g, w_in, attn_out_g, dw_w, dw_b, conv_ln_g, conv_ln_b, w_pw, conv_out_g,
           w_out, ple_norm_g, w_ple_gate, w_ple, final_g):
    batch, seq, _ = x.shape
    depth = w_in.shape[0]
    m = batch * seq
    h = x.reshape(m, D_MODEL)
    rows = lambda a: a.reshape(depth, 1, -1)
    g_pair = rows(jnp.concatenate([attn_out_g, attn_out_g], axis=-1))
    p3d = p.reshape(depth, m, PLE_DIM)
    for i in range(depth):
        q, k, v, sga, c, sgc = _in_proj(h, rows(norm_g), w_in, i)
        ya = _attention(q, k, v, sga, g_pair, i, batch, seq)
        yc = _conv_branch(c, sgc, dw_w, rows(dw_b), rows(conv_ln_g), rows(conv_ln_b), w_pw,
                          rows(conv_out_g), i, batch, seq)
        h = _out_proj(ya, yc, h, p3d, w_out, rows(ple_norm_g), w_ple_gate, w_ple,
                      final_g.reshape(1, -1), i, final=(i == depth - 1))
    return h.reshape(batch, seq, D_MODEL)
```

```python
import functools

import jax
import jax.numpy as jnp
from jax import lax
from jax.experimental import pallas as pl
from jax.experimental.pallas import tpu as pltpu

D_MODEL = 1024
ATTN_DIM = 512
CONV_DIM = 512
HEAD_DIM = 64
CONV_WIDTH = 31
PLE_DIM = 256
D_IN = 4 * ATTN_DIM + 3 * CONV_DIM
EPS = 1e-6

LANES = 128
SUBLANES = 8
K_CHUNK = 256
HEADS_IN_FLIGHT = 1
HALO = 32
ROW_TILE = 512
OUT_TILE = 1024
CONV_TILE = 512
CONV_ROWS = 128
VMEM_LIMIT = 56 * 1024 * 1024
LOG2E = 1.4426950408889634
EXP_UNDERFLOW = -104.0

F32 = jnp.float32
BF16 = jnp.bfloat16


def _sigmoid(x):
    return 1.0 / (1.0 + jnp.exp(-x))


def _cast_once(pairs, grid_rank=1):
    first = pl.program_id(0) == 0
    for axis in range(1, grid_rank):
        first = jnp.logical_and(first, pl.program_id(axis) == 0)

    @pl.when(first)
    def _():
        for src_ref, dst_ref in pairs:
            for c0 in range(0, src_ref.shape[1], ATTN_DIM):
                dst_ref[:, c0:c0 + ATTN_DIM] = src_ref[:, c0:c0 + ATTN_DIM].astype(BF16)


def _in_proj_kernel(x_ref, g_ref, w32_ref, q_ref, k_ref, v_ref, sga_ref, c_ref, sgc_ref, w_ref):
    _cast_once([(w32_ref, w_ref)])
    x = x_ref[...]
    ms = jnp.mean(x * x, axis=-1, keepdims=True)
    hn = (x * lax.rsqrt(ms + EPS) * g_ref[...]).astype(BF16)

    def seg(i):
        return jnp.dot(hn, w_ref[:, i * ATTN_DIM:(i + 1) * ATTN_DIM], preferred_element_type=F32)

    q_ref[...] = (seg(0) * (HEAD_DIM ** -0.5)).astype(BF16)
    k_ref[...] = seg(1).astype(BF16)
    v_ref[...] = seg(2).astype(BF16)
    ga = seg(3)
    sga_ref[...] = ga * _sigmoid(ga)
    c_ref[...] = seg(4) * _sigmoid(seg(5))
    gc = seg(6)
    sgc_ref[...] = gc * _sigmoid(gc)


def _layer_spec(layer, shape):
    zeros = (0,) * len(shape)
    return pl.BlockSpec((None,) + tuple(shape), lambda *_: (layer,) + zeros,
                        pipeline_mode=pl.Buffered(1))


def _in_proj(h2d, norm_g, w_in, layer):
    m = h2d.shape[0]
    row = lambda i: (i, 0)
    seg_spec = pl.BlockSpec((ROW_TILE, ATTN_DIM), row)
    return pl.pallas_call(
        _in_proj_kernel,
        grid=(m // ROW_TILE,),
        in_specs=[
            pl.BlockSpec((ROW_TILE, D_MODEL), row),
            _layer_spec(layer, (1, D_MODEL)),
            _layer_spec(layer, (D_MODEL, D_IN)),
        ],
        out_specs=[seg_spec] * 6,
        out_shape=[
            jax.ShapeDtypeStruct((m, ATTN_DIM), BF16),
            jax.ShapeDtypeStruct((m, ATTN_DIM), BF16),
            jax.ShapeDtypeStruct((m, ATTN_DIM), BF16),
            jax.ShapeDtypeStruct((m, ATTN_DIM), F32),
            jax.ShapeDtypeStruct((m, CONV_DIM), F32),
            jax.ShapeDtypeStruct((m, CONV_DIM), F32),
        ],
        scratch_shapes=[pltpu.VMEM((D_MODEL, D_IN), BF16)],
        compiler_params=pltpu.CompilerParams(
            dimension_semantics=("arbitrary",), vmem_limit_bytes=VMEM_LIMIT),
        name="in_proj",
    )(h2d, norm_g, w_in)


def _attn_kernel(q_ref, k_ref, v_ref, sg_ref, g_ref, hm_ref, w_ref, o_ref,
                 qm_ref, vm_ref, acc_ref, carry_ref, live_ref):
    seq = q_ref.shape[0]
    n_chunks = seq // K_CHUNK
    n_sub = K_CHUNK // LANES
    row = lax.broadcasted_iota(jnp.int32, (K_CHUNK, K_CHUNK), 0)
    col = lax.broadcasted_iota(jnp.int32, (K_CHUNK, K_CHUNK), 1)
    causal = col < row

    for h in range(2):
        qm_ref[h] = q_ref[...] * hm_ref[h:h + 1, :]
        for j in range(n_chunks):
            vm_ref[j, h * K_CHUNK:(h + 1) * K_CHUNK, :] = (
                v_ref[j * K_CHUNK:(j + 1) * K_CHUNK, :] * hm_ref[h:h + 1, :])

    def on_rows(m, has_diag, diag_fn, rest_fn):
        n_diag = K_CHUNK if has_diag else 0
        parts = [diag_fn(slice(0, n_diag))] if has_diag else []
        if m > n_diag:
            parts.append(rest_fn(slice(n_diag, m)))
        return parts[0] if len(parts) == 1 else jnp.concatenate(parts, axis=0)

    def head(q0, m, j, has_diag):
        kc = k_ref[pl.ds(j * K_CHUNK, K_CHUNK), :]
        zs, lhs = [], []
        for h in range(2):
            z = lax.dot_general(qm_ref[h, pl.ds(q0, m), :], kc, (((1,), (1,)), ((), ())),
                                preferred_element_type=F32)
            sp = jnp.maximum(z, 0.0) + jnp.log(1.0 + jnp.exp2(jnp.abs(z) * -LOG2E))
            sp = on_rows(m, has_diag, lambda d: jnp.where(causal, sp[d], 0.0), lambda b: sp[b])
            hi = sp.astype(BF16)
            lo = (sp - hi.astype(F32)).astype(BF16)
            zs.append(z)
            for s in reversed(range(n_sub)):
                sl = slice(s * LANES, (s + 1) * LANES)
                lhs.append(jnp.concatenate([hi[:, sl], lo[:, sl]], axis=1))
        r = jnp.dot(jnp.concatenate(lhs, axis=0), w_ref[...], preferred_element_type=F32)
        return q0, m, j, has_diag, zs, r

    def tail(q0, m, j, has_diag, zs, r):
        n_diag = K_CHUNK if has_diag else 0
        ps = []
        for h in range(2):
            z = zs[h]
            c = None
            p_sub = [None] * n_sub
            for idx, s in enumerate(reversed(range(n_sub))):
                sl = slice(s * LANES, (s + 1) * LANES)
                rr = r[(h * n_sub + idx) * m:(h * n_sub + idx + 1) * m]
                arg = z[:, sl] + rr[:, :LANES]
                if c is None:
                    c_in = carry_ref[h, pl.ds(q0 + n_diag, m - n_diag), :] if m > n_diag else None
                    arg = on_rows(m, has_diag, lambda d: arg[d], lambda b: arg[b] + c_in)
                    c = on_rows(m, has_diag, lambda d: rr[d, LANES:],
                                lambda b: c_in + rr[b, LANES:])
                else:
                    arg = arg + c
                    c = c + rr[:, LANES:]
                p_sub[s] = jnp.exp(arg)
            carry_ref[h, pl.ds(q0, m), :] = c
            p = jnp.concatenate(p_sub, axis=1)
            p = on_rows(m, has_diag, lambda d: jnp.where(causal, p[d], 0.0), lambda b: p[b])
            ps.append(p.astype(BF16))
        pv = jnp.dot(jnp.concatenate(ps, axis=1), vm_ref[j], preferred_element_type=F32)
        if has_diag:
            acc_ref[pl.ds(q0, n_diag), :] = pv[:n_diag]
        if m > n_diag:
            acc_ref[pl.ds(q0 + n_diag, m - n_diag), :] += pv[n_diag:]

    lane = lax.broadcasted_iota(jnp.int32, (K_CHUNK, LANES), 1)
    first_head = lane < HEAD_DIM

    def emit(q0):
        rows = pl.ds(q0, K_CHUNK)
        o = acc_ref[rows, :]
        o2 = o * o
        ms0 = jnp.sum(jnp.where(first_head, o2, 0.0), axis=1, keepdims=True) * (1.0 / HEAD_DIM)
        ms1 = jnp.sum(jnp.where(first_head, 0.0, o2), axis=1, keepdims=True) * (1.0 / HEAD_DIM)
        inv = jnp.where(first_head, lax.rsqrt(ms0 + EPS), lax.rsqrt(ms1 + EPS))
        y = o * inv * g_ref[...] * sg_ref[rows, :]
        o_ref[rows, :] = y.astype(BF16)

    def alive(q0):
        return jnp.max(carry_ref[:, pl.ds(q0, K_CHUNK), :]) > EXP_UNDERFLOW

    def finish(issued):
        tail(*issued)
        j = issued[2]
        if j + 1 < n_chunks:
            emit((j + 1) * K_CHUNK)
            live_ref[j + 1] = alive((j + 1) * K_CHUNK).astype(jnp.int32)

    pending = []
    for j in reversed(range(n_chunks)):
        pending.append(head(j * K_CHUNK, min(2 * K_CHUNK, seq - j * K_CHUNK), j, True))
        if len(pending) > HEADS_IN_FLIGHT:
            finish(pending.pop(0))
    for issued in pending:
        finish(issued)
    emit(0)

    def finish_block(i, _):
        q0 = pl.multiple_of(i * K_CHUNK, K_CHUNK)

        def more(state):
            j, live = state
            return jnp.logical_and(j >= 0, live)

        def step(state):
            j, _ = state
            tail(*head(q0, K_CHUNK, j, False))
            return j - 1, alive(q0)

        ran = live_ref[i] > 0
        lax.while_loop(more, step, (i - 2, ran))

        @pl.when(ran)
        def _():
            emit(q0)

        return 0

    lax.fori_loop(2, n_chunks, finish_block, 0)


def _cumsum_weights():
    j = lax.broadcasted_iota(jnp.int32, (LANES, LANES), 0)
    s = lax.broadcasted_iota(jnp.int32, (LANES, LANES), 1)
    neg_u = jnp.where(j >= s, -1.0, 0.0).astype(BF16)
    half = jnp.concatenate([neg_u, jnp.full((LANES, LANES), -1.0, BF16)], axis=1)
    return jnp.concatenate([half, half], axis=0)


def _head_masks():
    lane = lax.broadcasted_iota(jnp.int32, (2, LANES), 1)
    head = lax.broadcasted_iota(jnp.int32, (2, LANES), 0)
    return ((lane // HEAD_DIM) == head).astype(BF16)


def _attention(q, k, v, sg, g_pair, layer, batch, seq):
    n_pairs = ATTN_DIM // LANES
    seq_spec = pl.BlockSpec((seq, LANES), lambda b, p: (b, p))
    const = lambda b, p: (0, 0)
    return pl.pallas_call(
        _attn_kernel,
        grid=(batch, n_pairs),
        in_specs=[
            seq_spec, seq_spec, seq_spec, seq_spec,
            _layer_spec(layer, (1, LANES)),
            pl.BlockSpec((2, LANES), const),
            pl.BlockSpec((2 * LANES, 2 * LANES), const),
        ],
        out_specs=seq_spec,
        out_shape=jax.ShapeDtypeStruct((batch * seq, ATTN_DIM), BF16),
        scratch_shapes=[
            pltpu.VMEM((2, seq, LANES), BF16),
            pltpu.VMEM((seq // K_CHUNK, 2 * K_CHUNK, LANES), BF16),
            pltpu.VMEM((seq, LANES), F32),
            pltpu.VMEM((2, seq, LANES), F32),
            pltpu.SMEM((seq // K_CHUNK,), jnp.int32),
        ],
        compiler_params=pltpu.CompilerParams(
            dimension_semantics=("arbitrary", "arbitrary"), vmem_limit_bytes=VMEM_LIMIT),
        name="sb_attention",
    )(q, k, v, sg, g_pair, _head_masks(), _cumsum_weights())


def _conv_kernel(c_ref, halo_ref, sg_ref, dww_ref, dwb_ref, lng_ref, lnb_ref, wpw32_ref, og_ref,
                 o_ref, xw_ref, cv_ref, wpw_ref):
    _cast_once([(wpw32_ref, wpw_ref)], grid_rank=2)
    i = pl.program_id(1)
    halo = halo_ref[...]
    xw_ref[0:HALO, :] = jnp.where(i == 0, jnp.zeros_like(halo), halo)
    xw_ref[HALO:, :] = c_ref[...]

    base = HALO - (CONV_WIDTH - 1)
    span = HALO + CONV_ROWS
    for r0 in range(0, CONV_TILE, CONV_ROWS):
        for l0 in range(0, CONV_DIM, LANES):
            acc = jnp.zeros((CONV_ROWS, LANES), F32) + dwb_ref[:, l0:l0 + LANES]
            xs = xw_ref[r0:r0 + span, l0:l0 + LANES]
            for phase in range(SUBLANES):
                taps = [w for w in range(CONV_WIDTH) if (base + w) % SUBLANES == phase]
                if not taps:
                    continue
                y = xs if phase == 0 else pltpu.roll(xs, span - phase, axis=0)
                for w in taps:
                    a = base + w - phase
                    acc = acc + y[a:a + CONV_ROWS, :] * dww_ref[w:w + 1, l0:l0 + LANES]
            cv_ref[r0:r0 + CONV_ROWS, l0:l0 + LANES] = acc

    x = cv_ref[...]
    mu = jnp.mean(x, axis=-1, keepdims=True)
    xc = x - mu
    var = jnp.mean(xc * xc, axis=-1, keepdims=True)
    y = xc * lax.rsqrt(var + EPS) * lng_ref[...] + lnb_ref[...]
    y = y * _sigmoid(y)
    z = jnp.dot(y.astype(BF16), wpw_ref[...], preferred_element_type=F32)
    ms = jnp.mean(z * z, axis=-1, keepdims=True)
    out = z * lax.rsqrt(ms + EPS) * og_ref[...] * sg_ref[...]
    o_ref[...] = out.astype(BF16)


def _conv_branch(c, sg, dw_w, dw_b, ln_g, ln_b, w_pw, out_g, layer, batch, seq):
    tiles = seq // CONV_TILE
    halo_per_tile = CONV_TILE // HALO
    tile_spec = pl.BlockSpec((CONV_TILE, CONV_DIM), lambda b, i: (b * tiles + i, 0))
    halo_spec = pl.BlockSpec(
        (HALO, CONV_DIM),
        lambda b, i: (jnp.maximum((b * tiles + i) * halo_per_tile - 1, 0), 0))
    vec = _layer_spec(layer, (1, CONV_DIM))
    return pl.pallas_call(
        _conv_kernel,
        grid=(batch, tiles),
        in_specs=[
            tile_spec, halo_spec, tile_spec,
            _layer_spec(layer, (CONV_WIDTH, CONV_DIM)),
            vec, vec, vec,
            _layer_spec(layer, (CONV_DIM, CONV_DIM)),
            vec,
        ],
        out_specs=tile_spec,
        out_shape=jax.ShapeDtypeStruct((batch * seq, CONV_DIM), BF16),
        scratch_shapes=[
            pltpu.VMEM((CONV_TILE + HALO, CONV_DIM), F32),
            pltpu.VMEM((CONV_TILE, CONV_DIM), F32),
            pltpu.VMEM((CONV_DIM, CONV_DIM), BF16),
        ],
        compiler_params=pltpu.CompilerParams(
            dimension_semantics=("arbitrary", "arbitrary"), vmem_limit_bytes=VMEM_LIMIT),
        name="conv_branch",
    )(c, c, sg, dw_w, dw_b, ln_g, ln_b, w_pw, out_g)


def _out_proj_kernel(ya_ref, yc_ref, h_ref, p_ref, wo32_ref, ng_ref, wg32_ref, wp32_ref, fg_ref,
                     o_ref, wo_ref, wg_ref, wp_ref, *, final):
    _cast_once([(wo32_ref, wo_ref), (wg32_ref, wg_ref), (wp32_ref, wp_ref)])

    def mix(rows):
        h = h_ref[rows, :]
        h = h + jnp.dot(ya_ref[rows, :], wo_ref[0:ATTN_DIM, :], preferred_element_type=F32)
        h = h + jnp.dot(yc_ref[rows, :], wo_ref[ATTN_DIM:, :], preferred_element_type=F32)
        ms = jnp.mean(h * h, axis=-1, keepdims=True)
        return h, (h * lax.rsqrt(ms + EPS) * ng_ref[...]).astype(BF16)

    def embed(rows, h, hn):
        gate = _sigmoid(jnp.dot(hn, wg_ref[...], preferred_element_type=F32))
        e = jnp.dot(p_ref[rows, :].astype(BF16), wp_ref[...], preferred_element_type=F32)
        h = h + e * gate
        if final:
            ms = jnp.mean(h * h, axis=-1, keepdims=True)
            h = h * lax.rsqrt(ms + EPS) * fg_ref[...]
        o_ref[rows, :] = h

    n_rows = h_ref.shape[0]
    halves = [slice(0, n_rows // 2), slice(n_rows // 2, n_rows)]
    mixed = [mix(rows) for rows in halves]
    for rows, (h, hn) in zip(halves, mixed):
        embed(rows, h, hn)


def _out_proj(ya, yc, h2d, p3d, w_out, ple_norm_g, w_ple_gate, w_ple, final_g, layer, final):
    m = h2d.shape[0]
    row = lambda i: (i, 0)
    return pl.pallas_call(
        functools.partial(_out_proj_kernel, final=final),
        grid=(m // OUT_TILE,),
        in_specs=[
            pl.BlockSpec((OUT_TILE, ATTN_DIM), row),
            pl.BlockSpec((OUT_TILE, CONV_DIM), row),
            pl.BlockSpec((OUT_TILE, D_MODEL), row),
            pl.BlockSpec((None, OUT_TILE, PLE_DIM), lambda i: (layer, i, 0)),
            _layer_spec(layer, (D_MODEL, D_MODEL)),
            _layer_spec(layer, (1, D_MODEL)),
            _layer_spec(layer, (D_MODEL, D_MODEL)),
            _layer_spec(layer, (PLE_DIM, D_MODEL)),
            pl.BlockSpec((1, D_MODEL), lambda i: (0, 0)),
        ],
        out_specs=pl.BlockSpec((OUT_TILE, D_MODEL), row),
        out_shape=jax.ShapeDtypeStruct((m, D_MODEL), F32),
        scratch_shapes=[
            pltpu.VMEM((D_MODEL, D_MODEL), BF16),
            pltpu.VMEM((D_MODEL, D_MODEL), BF16),
            pltpu.VMEM((PLE_DIM, D_MODEL), BF16),
        ],
        compiler_params=pltpu.CompilerParams(
            dimension_semantics=("arbitrary",), vmem_limit_bytes=VMEM_LIMIT),
        name="out_proj_final" if final else "out_proj",
    )(ya, yc, h2d, p3d, w_out, ple_norm_g, w_ple_gate, w_ple, final_g)


def kernel(x, p, norm_g, w_in, attn_out_g, dw_w, dw_b, conv_ln_g, conv_ln_b, w_pw, conv_out_g,
           w_out, ple_norm_g, w_ple_gate, w_ple, final_g):
    batch, seq, _ = x.shape
    depth = w_in.shape[0]
    m = batch * seq
    h = x.reshape(m, D_MODEL)
    rows = lambda a: a.reshape(depth, 1, -1)
    g_pair = rows(jnp.concatenate([attn_out_g, attn_out_g], axis=-1))
    p3d = p.reshape(depth, m, PLE_DIM)
    for i in range(depth):
        q, k, v, sga, c, sgc = _in_proj(h, rows(norm_g), w_in, i)
        ya = _attention(q, k, v, sga, g_pair, i, batch, seq)
        yc = _conv_branch(c, sgc, dw_w, rows(dw_b), rows(conv_ln_g), rows(conv_ln_b), w_pw,
                          rows(conv_out_g), i, batch, seq)
        h = _out_proj(ya, yc, h, p3d, w_out, rows(ple_norm_g), w_ple_gate, w_ple,
                      final_g.reshape(1, -1), i, final=(i == depth - 1))
    return h.reshape(batch, seq, D_MODEL)
```

```python
import functools

import jax
import jax.numpy as jnp
from jax import lax
from jax.experimental import pallas as pl
from jax.experimental.pallas import tpu as pltpu

D_MODEL = 1024
ATTN_DIM = 512
CONV_DIM = 512
HEAD_DIM = 64
CONV_WIDTH = 31
PLE_DIM = 256
D_IN = 4 * ATTN_DIM + 3 * CONV_DIM
EPS = 1e-6

LANES = 128
SUBLANES = 8
K_CHUNK = 256
HEADS_IN_FLIGHT = 1
HALO = 32
ROW_TILE = 1024
OUT_TILE = 1024
CONV_TILE = 512
CONV_ROWS = 128
VMEM_LIMIT = 56 * 1024 * 1024
LOG2E = 1.4426950408889634
EXP_UNDERFLOW = -104.0

F32 = jnp.float32
BF16 = jnp.bfloat16


def _sigmoid(x):
    return 1.0 / (1.0 + jnp.exp(-x))


def _cast_once(pairs, grid_rank=1):
    first = pl.program_id(0) == 0
    for axis in range(1, grid_rank):
        first = jnp.logical_and(first, pl.program_id(axis) == 0)

    @pl.when(first)
    def _():
        for src_ref, dst_ref in pairs:
            for c0 in range(0, src_ref.shape[1], ATTN_DIM):
                dst_ref[:, c0:c0 + ATTN_DIM] = src_ref[:, c0:c0 + ATTN_DIM].astype(BF16)


def _in_proj_kernel(x_ref, g_ref, w32_ref, q_ref, k_ref, v_ref, sga_ref, c_ref, sgc_ref, w_ref):
    _cast_once([(w32_ref, w_ref)])

    def normed(rows):
        x = x_ref[rows, :]
        ms = jnp.mean(x * x, axis=-1, keepdims=True)
        return (x * lax.rsqrt(ms + EPS) * g_ref[...]).astype(BF16)

    def project(rows, hn):
        def seg(i):
            return jnp.dot(hn, w_ref[:, i * ATTN_DIM:(i + 1) * ATTN_DIM],
                           preferred_element_type=F32)

        q_ref[rows, :] = (seg(0) * (HEAD_DIM ** -0.5)).astype(BF16)
        k_ref[rows, :] = seg(1).astype(BF16)
        v_ref[rows, :] = seg(2).astype(BF16)
        ga = seg(3)
        sga_ref[rows, :] = ga * _sigmoid(ga)
        c_ref[rows, :] = seg(4) * _sigmoid(seg(5))
        gc = seg(6)
        sgc_ref[rows, :] = gc * _sigmoid(gc)

    n_rows = x_ref.shape[0]
    halves = [slice(0, n_rows // 2), slice(n_rows // 2, n_rows)]
    hns = [normed(rows) for rows in halves]
    for rows, hn in zip(halves, hns):
        project(rows, hn)


def _layer_spec(layer, shape):
    zeros = (0,) * len(shape)
    return pl.BlockSpec((None,) + tuple(shape), lambda *_: (layer,) + zeros,
                        pipeline_mode=pl.Buffered(1))


def _in_proj(h2d, norm_g, w_in, layer):
    m = h2d.shape[0]
    row = lambda i: (i, 0)
    seg_spec = pl.BlockSpec((ROW_TILE, ATTN_DIM), row)
    return pl.pallas_call(
        _in_proj_kernel,
        grid=(m // ROW_TILE,),
        in_specs=[
            pl.BlockSpec((ROW_TILE, D_MODEL), row),
            _layer_spec(layer, (1, D_MODEL)),
            _layer_spec(layer, (D_MODEL, D_IN)),
        ],
        out_specs=[seg_spec] * 6,
        out_shape=[
            jax.ShapeDtypeStruct((m, ATTN_DIM), BF16),
            jax.ShapeDtypeStruct((m, ATTN_DIM), BF16),
            jax.ShapeDtypeStruct((m, ATTN_DIM), BF16),
            jax.ShapeDtypeStruct((m, ATTN_DIM), F32),
            jax.ShapeDtypeStruct((m, CONV_DIM), F32),
            jax.ShapeDtypeStruct((m, CONV_DIM), F32),
        ],
        scratch_shapes=[pltpu.VMEM((D_MODEL, D_IN), BF16)],
        compiler_params=pltpu.CompilerParams(
            dimension_semantics=("arbitrary",), vmem_limit_bytes=VMEM_LIMIT),
        name="in_proj",
    )(h2d, norm_g, w_in)


def _attn_kernel(q_ref, k_ref, v_ref, sg_ref, g_ref, hm_ref, w_ref, o_ref,
                 qm_ref, vm_ref, acc_ref, carry_ref, live_ref):
    seq = q_ref.shape[0]
    n_chunks = seq // K_CHUNK
    n_sub = K_CHUNK // LANES
    row = lax.broadcasted_iota(jnp.int32, (K_CHUNK, K_CHUNK), 0)
    col = lax.broadcasted_iota(jnp.int32, (K_CHUNK, K_CHUNK), 1)
    causal = col < row

    for h in range(2):
        qm_ref[h] = q_ref[...] * hm_ref[h:h + 1, :]
        for j in range(n_chunks):
            vm_ref[j, h * K_CHUNK:(h + 1) * K_CHUNK, :] = (
                v_ref[j * K_CHUNK:(j + 1) * K_CHUNK, :] * hm_ref[h:h + 1, :])

    def on_rows(m, has_diag, diag_fn, rest_fn):
        n_diag = K_CHUNK if has_diag else 0
        parts = [diag_fn(slice(0, n_diag))] if has_diag else []
        if m > n_diag:
            parts.append(rest_fn(slice(n_diag, m)))
        return parts[0] if len(parts) == 1 else jnp.concatenate(parts, axis=0)

    def head(q0, m, j, has_diag):
        kc = k_ref[pl.ds(j * K_CHUNK, K_CHUNK), :]
        zs, lhs = [], []
        for h in range(2):
            z = lax.dot_general(qm_ref[h, pl.ds(q0, m), :], kc, (((1,), (1,)), ((), ())),
                                preferred_element_type=F32)
            sp = jnp.maximum(z, 0.0) + jnp.log(1.0 + jnp.exp2(jnp.abs(z) * -LOG2E))
            sp = on_rows(m, has_diag, lambda d: jnp.where(causal, sp[d], 0.0), lambda b: sp[b])
            hi = sp.astype(BF16)
            lo = (sp - hi.astype(F32)).astype(BF16)
            zs.append(z)
            for s in reversed(range(n_sub)):
                sl = slice(s * LANES, (s + 1) * LANES)
                lhs.append(jnp.concatenate([hi[:, sl], lo[:, sl]], axis=1))
        r = jnp.dot(jnp.concatenate(lhs, axis=0), w_ref[...], preferred_element_type=F32)
        return q0, m, j, has_diag, zs, r

    def tail(q0, m, j, has_diag, zs, r):
        n_diag = K_CHUNK if has_diag else 0
        ps = []
        for h in range(2):
            z = zs[h]
            c = None
            p_sub = [None] * n_sub
            for idx, s in enumerate(reversed(range(n_sub))):
                sl = slice(s * LANES, (s + 1) * LANES)
                rr = r[(h * n_sub + idx) * m:(h * n_sub + idx + 1) * m]
                arg = z[:, sl] + rr[:, :LANES]
                if c is None:
                    c_in = carry_ref[h, pl.ds(q0 + n_diag, m - n_diag), :] if m > n_diag else None
                    arg = on_rows(m, has_diag, lambda d: arg[d], lambda b: arg[b] + c_in)
                    c = on_rows(m, has_diag, lambda d: rr[d, LANES:],
                                lambda b: c_in + rr[b, LANES:])
                else:
                    arg = arg + c
                    c = c + rr[:, LANES:]
                p_sub[s] = jnp.exp(arg)
            carry_ref[h, pl.ds(q0, m), :] = c
            p = jnp.concatenate(p_sub, axis=1)
            p = on_rows(m, has_diag, lambda d: jnp.where(causal, p[d], 0.0), lambda b: p[b])
            ps.append(p.astype(BF16))
        pv = jnp.dot(jnp.concatenate(ps, axis=1), vm_ref[j], preferred_element_type=F32)
        if has_diag:
            acc_ref[pl.ds(q0, n_diag), :] = pv[:n_diag]
        if m > n_diag:
            acc_ref[pl.ds(q0 + n_diag, m - n_diag), :] += pv[n_diag:]

    lane = lax.broadcasted_iota(jnp.int32, (K_CHUNK, LANES), 1)
    first_head = lane < HEAD_DIM

    def emit(q0):
        rows = pl.ds(q0, K_CHUNK)
        o = acc_ref[rows, :]
        o2 = o * o
        ms0 = jnp.sum(jnp.where(first_head, o2, 0.0), axis=1, keepdims=True) * (1.0 / HEAD_DIM)
        ms1 = jnp.sum(jnp.where(first_head, 0.0, o2), axis=1, keepdims=True) * (1.0 / HEAD_DIM)
        inv = jnp.where(first_head, lax.rsqrt(ms0 + EPS), lax.rsqrt(ms1 + EPS))
        y = o * inv * g_ref[...] * sg_ref[rows, :]
        o_ref[rows, :] = y.astype(BF16)

    def alive(q0):
        return jnp.max(carry_ref[:, pl.ds(q0, K_CHUNK), :]) > EXP_UNDERFLOW

    def finish(issued):
        tail(*issued)
        j = issued[2]
        if j + 1 < n_chunks:
            emit((j + 1) * K_CHUNK)
            live_ref[j + 1] = alive((j + 1) * K_CHUNK).astype(jnp.int32)

    pending = []
    for j in reversed(range(n_chunks)):
        pending.append(head(j * K_CHUNK, min(2 * K_CHUNK, seq - j * K_CHUNK), j, True))
        if len(pending) > HEADS_IN_FLIGHT:
            finish(pending.pop(0))
    for issued in pending:
        finish(issued)
    emit(0)

    def finish_block(i, _):
        q0 = pl.multiple_of(i * K_CHUNK, K_CHUNK)

        def more(state):
            j, live = state
            return jnp.logical_and(j >= 0, live)

        def step(state):
            j, _ = state
            tail(*head(q0, K_CHUNK, j, False))
            return j - 1, alive(q0)

        ran = live_ref[i] > 0
        lax.while_loop(more, step, (i - 2, ran))

        @pl.when(ran)
        def _():
            emit(q0)

        return 0

    lax.fori_loop(2, n_chunks, finish_block, 0)


def _cumsum_weights():
    j = lax.broadcasted_iota(jnp.int32, (LANES, LANES), 0)
    s = lax.broadcasted_iota(jnp.int32, (LANES, LANES), 1)
    neg_u = jnp.where(j >= s, -1.0, 0.0).astype(BF16)
    half = jnp.concatenate([neg_u, jnp.full((LANES, LANES), -1.0, BF16)], axis=1)
    return jnp.concatenate([half, half], axis=0)


def _head_masks():
    lane = lax.broadcasted_iota(jnp.int32, (2, LANES), 1)
    head = lax.broadcasted_iota(jnp.int32, (2, LANES), 0)
    return ((lane // HEAD_DIM) == head).astype(BF16)


def _attention(q, k, v, sg, g_pair, layer, batch, seq):
    n_pairs = ATTN_DIM // LANES
    seq_spec = pl.BlockSpec((seq, LANES), lambda b, p: (b, p))
    const = lambda b, p: (0, 0)
    return pl.pallas_call(
        _attn_kernel,
        grid=(batch, n_pairs),
        in_specs=[
            seq_spec, seq_spec, seq_spec, seq_spec,
            _layer_spec(layer, (1, LANES)),
            pl.BlockSpec((2, LANES), const),
            pl.BlockSpec((2 * LANES, 2 * LANES), const),
        ],
        out_specs=seq_spec,
        out_shape=jax.ShapeDtypeStruct((batch * seq, ATTN_DIM), BF16),
        scratch_shapes=[
            pltpu.VMEM((2, seq, LANES), BF16),
            pltpu.VMEM((seq // K_CHUNK, 2 * K_CHUNK, LANES), BF16),
            pltpu.VMEM((seq, LANES), F32),
            pltpu.VMEM((2, seq, LANES), F32),
            pltpu.SMEM((seq // K_CHUNK,), jnp.int32),
        ],
        compiler_params=pltpu.CompilerParams(
            dimension_semantics=("arbitrary", "arbitrary"), vmem_limit_bytes=VMEM_LIMIT),
        name="sb_attention",
    )(q, k, v, sg, g_pair, _head_masks(), _cumsum_weights())


def _conv_kernel(c_ref, halo_ref, dww_ref, dwb_ref, o_ref, xw_ref):
    i = pl.program_id(1)
    halo = halo_ref[...]
    xw_ref[0:HALO, :] = jnp.where(i == 0, jnp.zeros_like(halo), halo)
    xw_ref[HALO:, :] = c_ref[...]

    base = HALO - (CONV_WIDTH - 1)
    span = HALO + CONV_ROWS
    for r0 in range(0, CONV_TILE, CONV_ROWS):
        for l0 in range(0, CONV_DIM, LANES):
            acc = jnp.zeros((CONV_ROWS, LANES), F32) + dwb_ref[:, l0:l0 + LANES]
            xs = xw_ref[r0:r0 + span, l0:l0 + LANES]
            for phase in range(SUBLANES):
                taps = [w for w in range(CONV_WIDTH) if (base + w) % SUBLANES == phase]
                if not taps:
                    continue
                y = xs if phase == 0 else pltpu.roll(xs, span - phase, axis=0)
                for w in taps:
                    a = base + w - phase
                    acc = acc + y[a:a + CONV_ROWS, :] * dww_ref[w:w + 1, l0:l0 + LANES]
            o_ref[r0:r0 + CONV_ROWS, l0:l0 + LANES] = acc


def _depthwise_conv(c, dw_w, dw_b, layer, batch, seq):
    tiles = seq // CONV_TILE
    halo_per_tile = CONV_TILE // HALO
    tile_spec = pl.BlockSpec((CONV_TILE, CONV_DIM), lambda b, i: (b * tiles + i, 0))
    halo_spec = pl.BlockSpec(
        (HALO, CONV_DIM),
        lambda b, i: (jnp.maximum((b * tiles + i) * halo_per_tile - 1, 0), 0))
    return pl.pallas_call(
        _conv_kernel,
        grid=(batch, tiles),
        in_specs=[
            tile_spec, halo_spec,
            _layer_spec(layer, (CONV_WIDTH, CONV_DIM)),
            _layer_spec(layer, (1, CONV_DIM)),
        ],
        out_specs=tile_spec,
        out_shape=jax.ShapeDtypeStruct((batch * seq, CONV_DIM), F32),
        scratch_shapes=[pltpu.VMEM((CONV_TILE + HALO, CONV_DIM), F32)],
        compiler_params=pltpu.CompilerParams(
            dimension_semantics=("arbitrary", "arbitrary"), vmem_limit_bytes=VMEM_LIMIT),
        name="depthwise_conv",
    )(c, c, dw_w, dw_b)


def _out_proj_kernel(ya_ref, cv_ref, sgc_ref, h_ref, p_ref, lng_ref, lnb_ref, wpw32_ref, og_ref,
                     wo32_ref, ng_ref, wg32_ref, wp32_ref, fg_ref,
                     o_ref, wpw_ref, wo_ref, wg_ref, wp_ref, *, final):
    _cast_once([(wpw32_ref, wpw_ref), (wo32_ref, wo_ref), (wg32_ref, wg_ref), (wp32_ref, wp_ref)])

    def conv_tail(rows):
        c = cv_ref[rows, :]
        mu = jnp.mean(c, axis=-1, keepdims=True)
        cc = c - mu
        var = jnp.mean(cc * cc, axis=-1, keepdims=True)
        y = cc * lax.rsqrt(var + EPS) * lng_ref[...] + lnb_ref[...]
        y = y * _sigmoid(y)
        z = jnp.dot(y.astype(BF16), wpw_ref[...], preferred_element_type=F32)
        ms = jnp.mean(z * z, axis=-1, keepdims=True)
        return (z * lax.rsqrt(ms + EPS) * og_ref[...] * sgc_ref[rows, :]).astype(BF16)

    def mix(rows):
        h = h_ref[rows, :]
        h = h + jnp.dot(ya_ref[rows, :], wo_ref[0:ATTN_DIM, :], preferred_element_type=F32)
        h = h + jnp.dot(conv_tail(rows), wo_ref[ATTN_DIM:, :], preferred_element_type=F32)
        ms = jnp.mean(h * h, axis=-1, keepdims=True)
        return h, (h * lax.rsqrt(ms + EPS) * ng_ref[...]).astype(BF16)

    def embed(rows, h, hn):
        gate = _sigmoid(jnp.dot(hn, wg_ref[...], preferred_element_type=F32))
        e = jnp.dot(p_ref[rows, :].astype(BF16), wp_ref[...], preferred_element_type=F32)
        h = h + e * gate
        if final:
            ms = jnp.mean(h * h, axis=-1, keepdims=True)
            h = h * lax.rsqrt(ms + EPS) * fg_ref[...]
        o_ref[rows, :] = h

    n_rows = h_ref.shape[0]
    halves = [slice(0, n_rows // 2), slice(n_rows // 2, n_rows)]
    mixed = [mix(rows) for rows in halves]
    for rows, (h, hn) in zip(halves, mixed):
        embed(rows, h, hn)


def _out_proj(ya, cv, sgc, h2d, p3d, ln_g, ln_b, w_pw, conv_out_g, w_out, ple_norm_g, w_ple_gate,
              w_ple, final_g, layer, final):
    m = h2d.shape[0]
    row = lambda i: (i, 0)
    conv_vec = _layer_spec(layer, (1, CONV_DIM))
    return pl.pallas_call(
        functools.partial(_out_proj_kernel, final=final),
        grid=(m // OUT_TILE,),
        in_specs=[
            pl.BlockSpec((OUT_TILE, ATTN_DIM), row),
            pl.BlockSpec((OUT_TILE, CONV_DIM), row),
            pl.BlockSpec((OUT_TILE, CONV_DIM), row),
            pl.BlockSpec((OUT_TILE, D_MODEL), row),
            pl.BlockSpec((None, OUT_TILE, PLE_DIM), lambda i: (layer, i, 0)),
            conv_vec, conv_vec,
            _layer_spec(layer, (CONV_DIM, CONV_DIM)),
            conv_vec,
            _layer_spec(layer, (D_MODEL, D_MODEL)),
            _layer_spec(layer, (1, D_MODEL)),
            _layer_spec(layer, (D_MODEL, D_MODEL)),
            _layer_spec(layer, (PLE_DIM, D_MODEL)),
            pl.BlockSpec((1, D_MODEL), lambda i: (0, 0)),
        ],
        out_specs=pl.BlockSpec((OUT_TILE, D_MODEL), row),
        out_shape=jax.ShapeDtypeStruct((m, D_MODEL), F32),
        scratch_shapes=[
            pltpu.VMEM((CONV_DIM, CONV_DIM), BF16),
            pltpu.VMEM((D_MODEL, D_MODEL), BF16),
            pltpu.VMEM((D_MODEL, D_MODEL), BF16),
            pltpu.VMEM((PLE_DIM, D_MODEL), BF16),
        ],
        compiler_params=pltpu.CompilerParams(
            dimension_semantics=("arbitrary",), vmem_limit_bytes=VMEM_LIMIT),
        name="out_proj_final" if final else "out_proj",
    )(ya, cv, sgc, h2d, p3d, ln_g, ln_b, w_pw, conv_out_g, w_out, ple_norm_g, w_ple_gate, w_ple,
      final_g)


def kernel(x, p, norm_g, w_in, attn_out_g, dw_w, dw_b, conv_ln_g, conv_ln_b, w_pw, conv_out_g,
           w_out, ple_norm_g, w_ple_gate, w_ple, final_g):
    batch, seq, _ = x.shape
    depth = w_in.shape[0]
    m = batch * seq
    h = x.reshape(m, D_MODEL)
    rows = lambda a: a.reshape(depth, 1, -1)
    g_pair = rows(jnp.concatenate([attn_out_g, attn_out_g], axis=-1))
    p3d = p.reshape(depth, m, PLE_DIM)
    for i in range(depth):
        q, k, v, sga, c, sgc = _in_proj(h, rows(norm_g), w_in, i)
        ya = _attention(q, k, v, sga, g_pair, i, batch, seq)
        cv = _depthwise_conv(c, dw_w, rows(dw_b), i, batch, seq)
        h = _out_proj(ya, cv, sgc, h, p3d, rows(conv_ln_g), rows(conv_ln_b), w_pw,
                      rows(conv_out_g), w_out, rows(ple_norm_g), w_ple_gate, w_ple,
                      final_g.reshape(1, -1), i, final=(i == depth - 1))
    return h.reshape(batch, seq, D_MODEL)
```

```python
import functools

import jax
import jax.numpy as jnp
from jax import lax
from jax.experimental import pallas as pl
from jax.experimental.pallas import tpu as pltpu

D_MODEL = 1024
ATTN_DIM = 512
CONV_DIM = 512
HEAD_DIM = 64
CONV_WIDTH = 31
PLE_DIM = 256
D_IN = 4 * ATTN_DIM + 3 * CONV_DIM
EPS = 1e-6

LANES = 128
SUBLANES = 8
K_CHUNK = 256
HEADS_IN_FLIGHT = 1
HALO = 32
ROW_TILE = 1024
OUT_TILE = 1024
CONV_TILE = 512
CONV_ROWS = 128
VMEM_LIMIT = 56 * 1024 * 1024
LOG2E = 1.4426950408889634
EXP_UNDERFLOW = -104.0

F32 = jnp.float32
BF16 = jnp.bfloat16


def _sigmoid(x):
    return 1.0 / (1.0 + jnp.exp(-x))


def _cast_once(pairs, grid_rank=1):
    first = pl.program_id(0) == 0
    for axis in range(1, grid_rank):
        first = jnp.logical_and(first, pl.program_id(axis) == 0)

    @pl.when(first)
    def _():
        for src_ref, dst_ref in pairs:
            for c0 in range(0, src_ref.shape[1], ATTN_DIM):
                dst_ref[:, c0:c0 + ATTN_DIM] = src_ref[:, c0:c0 + ATTN_DIM].astype(BF16)


def _in_proj_kernel(x_ref, g_ref, w32_ref, q0_ref, q1_ref, k_ref, v0_ref, v1_ref, sga_ref, c_ref,
                    sgc_ref, w_ref):
    _cast_once([(w32_ref, w_ref)])
    lane = lax.broadcasted_iota(jnp.int32, (1, ATTN_DIM), 1)
    even_head = (lane // HEAD_DIM) % 2 == 0

    def split_heads(t, even_ref, odd_ref, rows):
        even_ref[rows, :] = jnp.where(even_head, t, 0.0).astype(BF16)
        odd_ref[rows, :] = jnp.where(even_head, 0.0, t).astype(BF16)

    def normed(rows):
        x = x_ref[rows, :]
        ms = jnp.mean(x * x, axis=-1, keepdims=True)
        return (x * lax.rsqrt(ms + EPS) * g_ref[...]).astype(BF16)

    def project(rows, hn):
        def seg(i):
            return jnp.dot(hn, w_ref[:, i * ATTN_DIM:(i + 1) * ATTN_DIM],
                           preferred_element_type=F32)

        split_heads(seg(0) * (HEAD_DIM ** -0.5), q0_ref, q1_ref, rows)
        k_ref[rows, :] = seg(1).astype(BF16)
        split_heads(seg(2), v0_ref, v1_ref, rows)
        ga = seg(3)
        sga_ref[rows, :] = ga * _sigmoid(ga)
        c_ref[rows, :] = seg(4) * _sigmoid(seg(5))
        gc = seg(6)
        sgc_ref[rows, :] = gc * _sigmoid(gc)

    n_rows = x_ref.shape[0]
    halves = [slice(0, n_rows // 2), slice(n_rows // 2, n_rows)]
    hns = [normed(rows) for rows in halves]
    for rows, hn in zip(halves, hns):
        project(rows, hn)


def _layer_spec(layer, shape):
    zeros = (0,) * len(shape)
    return pl.BlockSpec((None,) + tuple(shape), lambda *_: (layer,) + zeros,
                        pipeline_mode=pl.Buffered(1))


def _in_proj(h2d, norm_g, w_in, layer):
    m = h2d.shape[0]
    row = lambda i: (i, 0)
    seg_spec = pl.BlockSpec((ROW_TILE, ATTN_DIM), row)
    return pl.pallas_call(
        _in_proj_kernel,
        grid=(m // ROW_TILE,),
        in_specs=[
            pl.BlockSpec((ROW_TILE, D_MODEL), row),
            _layer_spec(layer, (1, D_MODEL)),
            _layer_spec(layer, (D_MODEL, D_IN)),
        ],
        out_specs=[seg_spec] * 8,
        out_shape=[
            jax.ShapeDtypeStruct((m, ATTN_DIM), BF16),
            jax.ShapeDtypeStruct((m, ATTN_DIM), BF16),
            jax.ShapeDtypeStruct((m, ATTN_DIM), BF16),
            jax.ShapeDtypeStruct((m, ATTN_DIM), BF16),
            jax.ShapeDtypeStruct((m, ATTN_DIM), BF16),
            jax.ShapeDtypeStruct((m, ATTN_DIM), F32),
            jax.ShapeDtypeStruct((m, CONV_DIM), F32),
            jax.ShapeDtypeStruct((m, CONV_DIM), F32),
        ],
        scratch_shapes=[pltpu.VMEM((D_MODEL, D_IN), BF16)],
        compiler_params=pltpu.CompilerParams(
            dimension_semantics=("arbitrary",), vmem_limit_bytes=VMEM_LIMIT),
        name="in_proj",
    )(h2d, norm_g, w_in)


def _attn_kernel(q0_ref, q1_ref, k_ref, v0_ref, v1_ref, sg_ref, g_ref, w_ref, o_ref,
                 acc_ref, carry_ref, live_ref):
    seq = k_ref.shape[0]
    n_chunks = seq // K_CHUNK
    n_sub = K_CHUNK // LANES
    row = lax.broadcasted_iota(jnp.int32, (K_CHUNK, K_CHUNK), 0)
    col = lax.broadcasted_iota(jnp.int32, (K_CHUNK, K_CHUNK), 1)
    causal = col < row
    q_refs = (q0_ref, q1_ref)

    def on_rows(m, has_diag, diag_fn, rest_fn):
        n_diag = K_CHUNK if has_diag else 0
        parts = [diag_fn(slice(0, n_diag))] if has_diag else []
        if m > n_diag:
            parts.append(rest_fn(slice(n_diag, m)))
        return parts[0] if len(parts) == 1 else jnp.concatenate(parts, axis=0)

    def head(q0, m, j, has_diag):
        kc = k_ref[pl.ds(j * K_CHUNK, K_CHUNK), :]
        zs, lhs = [], []
        for h in range(2):
            z = lax.dot_general(q_refs[h][pl.ds(q0, m), :], kc, (((1,), (1,)), ((), ())),
                                preferred_element_type=F32)
            sp = jnp.maximum(z, 0.0) + jnp.log(1.0 + jnp.exp2(jnp.abs(z) * -LOG2E))
            sp = on_rows(m, has_diag, lambda d: jnp.where(causal, sp[d], 0.0), lambda b: sp[b])
            hi = sp.astype(BF16)
            lo = (sp - hi.astype(F32)).astype(BF16)
            zs.append(z)
            for s in reversed(range(n_sub)):
                sl = slice(s * LANES, (s + 1) * LANES)
                lhs.append(jnp.concatenate([hi[:, sl], lo[:, sl]], axis=1))
        r = jnp.dot(jnp.concatenate(lhs, axis=0), w_ref[...], preferred_element_type=F32)
        return q0, m, j, has_diag, zs, r

    def tail(q0, m, j, has_diag, zs, r):
        n_diag = K_CHUNK if has_diag else 0
        ps = []
        for h in range(2):
            z = zs[h]
            c = None
            p_sub = [None] * n_sub
            for idx, s in enumerate(reversed(range(n_sub))):
                sl = slice(s * LANES, (s + 1) * LANES)
                rr = r[(h * n_sub + idx) * m:(h * n_sub + idx + 1) * m]
                arg = z[:, sl] + rr[:, :LANES]
                if c is None:
                    c_in = carry_ref[h, pl.ds(q0 + n_diag, m - n_diag), :] if m > n_diag else None
                    arg = on_rows(m, has_diag, lambda d: arg[d], lambda b: arg[b] + c_in)
                    c = on_rows(m, has_diag, lambda d: rr[d, LANES:],
                                lambda b: c_in + rr[b, LANES:])
                else:
                    arg = arg + c
                    c = c + rr[:, LANES:]
                p_sub[s] = jnp.exp(arg)
            carry_ref[h, pl.ds(q0, m), :] = c
            p = jnp.concatenate(p_sub, axis=1)
            p = on_rows(m, has_diag, lambda d: jnp.where(causal, p[d], 0.0), lambda b: p[b])
            ps.append(p.astype(BF16))
        keys = pl.ds(j * K_CHUNK, K_CHUNK)
        vj = jnp.concatenate([v0_ref[keys, :], v1_ref[keys, :]], axis=0)
        pv = jnp.dot(jnp.concatenate(ps, axis=1), vj, preferred_element_type=F32)
        if has_diag:
            acc_ref[pl.ds(q0, n_diag), :] = pv[:n_diag]
        if m > n_diag:
            acc_ref[pl.ds(q0 + n_diag, m - n_diag), :] += pv[n_diag:]

    lane = lax.broadcasted_iota(jnp.int32, (K_CHUNK, LANES), 1)
    first_head = lane < HEAD_DIM

    def emit(q0):
        rows = pl.ds(q0, K_CHUNK)
        o = acc_ref[rows, :]
        o2 = o * o
        ms0 = jnp.sum(jnp.where(first_head, o2, 0.0), axis=1, keepdims=True) * (1.0 / HEAD_DIM)
        ms1 = jnp.sum(jnp.where(first_head, 0.0, o2), axis=1, keepdims=True) * (1.0 / HEAD_DIM)
        inv = jnp.where(first_head, lax.rsqrt(ms0 + EPS), lax.rsqrt(ms1 + EPS))
        y = o * inv * g_ref[...] * sg_ref[rows, :]
        o_ref[rows, :] = y.astype(BF16)

    def alive(q0):
        return jnp.max(carry_ref[:, pl.ds(q0, K_CHUNK), :]) > EXP_UNDERFLOW

    def finish(issued):
        tail(*issued)
        j = issued[2]
        if j + 1 < n_chunks:
            emit((j + 1) * K_CHUNK)
            live_ref[j + 1] = alive((j + 1) * K_CHUNK).astype(jnp.int32)

    pending = []
    for j in reversed(range(n_chunks)):
        pending.append(head(j * K_CHUNK, min(2 * K_CHUNK, seq - j * K_CHUNK), j, True))
        if len(pending) > HEADS_IN_FLIGHT:
            finish(pending.pop(0))
    for issued in pending:
        finish(issued)
    emit(0)

    def finish_block(i, _):
        q0 = pl.multiple_of(i * K_CHUNK, K_CHUNK)

        def more(state):
            j, live = state
            return jnp.logical_and(j >= 0, live)

        def step(state):
            j, _ = state
            tail(*head(q0, K_CHUNK, j, False))
            return j - 1, alive(q0)

        ran = live_ref[i] > 0
        lax.while_loop(more, step, (i - 2, ran))

        @pl.when(ran)
        def _():
            emit(q0)

        return 0

    lax.fori_loop(2, n_chunks, finish_block, 0)


def _cumsum_weights():
    j = lax.broadcasted_iota(jnp.int32, (LANES, LANES), 0)
    s = lax.broadcasted_iota(jnp.int32, (LANES, LANES), 1)
    neg_u = jnp.where(j >= s, -1.0, 0.0).astype(BF16)
    half = jnp.concatenate([neg_u, jnp.full((LANES, LANES), -1.0, BF16)], axis=1)
    return jnp.concatenate([half, half], axis=0)


def _attention(q0, q1, k, v0, v1, sg, g_pair, layer, batch, seq):
    n_pairs = ATTN_DIM // LANES
    seq_spec = pl.BlockSpec((seq, LANES), lambda b, p: (b, p))
    const = lambda b, p: (0, 0)
    return pl.pallas_call(
        _attn_kernel,
        grid=(batch, n_pairs),
        in_specs=[
            seq_spec, seq_spec, seq_spec, seq_spec, seq_spec, seq_spec,
            _layer_spec(layer, (1, LANES)),
            pl.BlockSpec((2 * LANES, 2 * LANES), const),
        ],
        out_specs=seq_spec,
        out_shape=jax.ShapeDtypeStruct((batch * seq, ATTN_DIM), BF16),
        scratch_shapes=[
            pltpu.VMEM((seq, LANES), F32),
            pltpu.VMEM((2, seq, LANES), F32),
            pltpu.SMEM((seq // K_CHUNK,), jnp.int32),
        ],
        compiler_params=pltpu.CompilerParams(
            dimension_semantics=("arbitrary", "arbitrary"), vmem_limit_bytes=VMEM_LIMIT),
        name="sb_attention",
    )(q0, q1, k, v0, v1, sg, g_pair, _cumsum_weights())


def _conv_kernel(c_ref, halo_ref, dww_ref, dwb_ref, o_ref, xw_ref):
    i = pl.program_id(1)
    halo = halo_ref[...]
    xw_ref[0:HALO, :] = jnp.where(i == 0, jnp.zeros_like(halo), halo)
    xw_ref[HALO:, :] = c_ref[...]

    base = HALO - (CONV_WIDTH - 1)
    span = HALO + CONV_ROWS
    for r0 in range(0, CONV_TILE, CONV_ROWS):
        for l0 in range(0, CONV_DIM, LANES):
            acc = jnp.zeros((CONV_ROWS, LANES), F32) + dwb_ref[:, l0:l0 + LANES]
            xs = xw_ref[r0:r0 + span, l0:l0 + LANES]
            for phase in range(SUBLANES):
                taps = [w for w in range(CONV_WIDTH) if (base + w) % SUBLANES == phase]
                if not taps:
                    continue
                y = xs if phase == 0 else pltpu.roll(xs, span - phase, axis=0)
                for w in taps:
                    a = base + w - phase
                    acc = acc + y[a:a + CONV_ROWS, :] * dww_ref[w:w + 1, l0:l0 + LANES]
            o_ref[r0:r0 + CONV_ROWS, l0:l0 + LANES] = acc


def _depthwise_conv(c, dw_w, dw_b, layer, batch, seq):
    tiles = seq // CONV_TILE
    halo_per_tile = CONV_TILE // HALO
    tile_spec = pl.BlockSpec((CONV_TILE, CONV_DIM), lambda b, i: (b * tiles + i, 0))
    halo_spec = pl.BlockSpec(
        (HALO, CONV_DIM),
        lambda b, i: (jnp.maximum((b * tiles + i) * halo_per_tile - 1, 0), 0))
    return pl.pallas_call(
        _conv_kernel,
        grid=(batch, tiles),
        in_specs=[
            tile_spec, halo_spec,
            _layer_spec(layer, (CONV_WIDTH, CONV_DIM)),
            _layer_spec(layer, (1, CONV_DIM)),
        ],
        out_specs=tile_spec,
        out_shape=jax.ShapeDtypeStruct((batch * seq, CONV_DIM), F32),
        scratch_shapes=[pltpu.VMEM((CONV_TILE + HALO, CONV_DIM), F32)],
        compiler_params=pltpu.CompilerParams(
            dimension_semantics=("arbitrary", "arbitrary"), vmem_limit_bytes=VMEM_LIMIT),
        name="depthwise_conv",
    )(c, c, dw_w, dw_b)


def _out_proj_kernel(ya_ref, cv_ref, sgc_ref, h_ref, p_ref, lng_ref, lnb_ref, wpw32_ref, og_ref,
                     wo32_ref, ng_ref, wg32_ref, wp32_ref, fg_ref,
                     o_ref, wpw_ref, wo_ref, wg_ref, wp_ref, *, final):
    _cast_once([(wpw32_ref, wpw_ref), (wo32_ref, wo_ref), (wg32_ref, wg_ref), (wp32_ref, wp_ref)])

    def conv_tail(rows):
        c = cv_ref[rows, :]
        mu = jnp.mean(c, axis=-1, keepdims=True)
        cc = c - mu
        var = jnp.mean(cc * cc, axis=-1, keepdims=True)
        y = cc * lax.rsqrt(var + EPS) * lng_ref[...] + lnb_ref[...]
        y = y * _sigmoid(y)
        z = jnp.dot(y.astype(BF16), wpw_ref[...], preferred_element_type=F32)
        ms = jnp.mean(z * z, axis=-1, keepdims=True)
        return (z * lax.rsqrt(ms + EPS) * og_ref[...] * sgc_ref[rows, :]).astype(BF16)

    def mix(rows):
        h = h_ref[rows, :]
        h = h + jnp.dot(ya_ref[rows, :], wo_ref[0:ATTN_DIM, :], preferred_element_type=F32)
        h = h + jnp.dot(conv_tail(rows), wo_ref[ATTN_DIM:, :], preferred_element_type=F32)
        ms = jnp.mean(h * h, axis=-1, keepdims=True)
        return h, (h * lax.rsqrt(ms + EPS) * ng_ref[...]).astype(BF16)

    def embed(rows, h, hn):
        gate = _sigmoid(jnp.dot(hn, wg_ref[...], preferred_element_type=F32))
        e = jnp.dot(p_ref[rows, :].astype(BF16), wp_ref[...], preferred_element_type=F32)
        h = h + e * gate
        if final:
            ms = jnp.mean(h * h, axis=-1, keepdims=True)
            h = h * lax.rsqrt(ms + EPS) * fg_ref[...]
        o_ref[rows, :] = h

    n_rows = h_ref.shape[0]
    halves = [slice(0, n_rows // 2), slice(n_rows // 2, n_rows)]
    mixed = [mix(rows) for rows in halves]
    for rows, (h, hn) in zip(halves, mixed):
        embed(rows, h, hn)


def _out_proj(ya, cv, sgc, h2d, p3d, ln_g, ln_b, w_pw, conv_out_g, w_out, ple_norm_g, w_ple_gate,
              w_ple, final_g, layer, final):
    m = h2d.shape[0]
    row = lambda i: (i, 0)
    conv_vec = _layer_spec(layer, (1, CONV_DIM))
    return pl.pallas_call(
        functools.partial(_out_proj_kernel, final=final),
        grid=(m // OUT_TILE,),
        in_specs=[
            pl.BlockSpec((OUT_TILE, ATTN_DIM), row),
            pl.BlockSpec((OUT_TILE, CONV_DIM), row),
            pl.BlockSpec((OUT_TILE, CONV_DIM), row),
            pl.BlockSpec((OUT_TILE, D_MODEL), row),
            pl.BlockSpec((None, OUT_TILE, PLE_DIM), lambda i: (layer, i, 0)),
            conv_vec, conv_vec,
            _layer_spec(layer, (CONV_DIM, CONV_DIM)),
            conv_vec,
            _layer_spec(layer, (D_MODEL, D_MODEL)),
            _layer_spec(layer, (1, D_MODEL)),
            _layer_spec(layer, (D_MODEL, D_MODEL)),
            _layer_spec(layer, (PLE_DIM, D_MODEL)),
            pl.BlockSpec((1, D_MODEL), lambda i: (0, 0)),
        ],
        out_specs=pl.BlockSpec((OUT_TILE, D_MODEL), row),
        out_shape=jax.ShapeDtypeStruct((m, D_MODEL), F32),
        scratch_shapes=[
            pltpu.VMEM((CONV_DIM, CONV_DIM), BF16),
            pltpu.VMEM((D_MODEL, D_MODEL), BF16),
            pltpu.VMEM((D_MODEL, D_MODEL), BF16),
            pltpu.VMEM((PLE_DIM, D_MODEL), BF16),
        ],
        compiler_params=pltpu.CompilerParams(
            dimension_semantics=("arbitrary",), vmem_limit_bytes=VMEM_LIMIT),
        name="out_proj_final" if final else "out_proj",
    )(ya, cv, sgc, h2d, p3d, ln_g, ln_b, w_pw, conv_out_g, w_out, ple_norm_g, w_ple_gate, w_ple,
      final_g)


def kernel(x, p, norm_g, w_in, attn_out_g, dw_w, dw_b, conv_ln_g, conv_ln_b, w_pw, conv_out_g,
           w_out, ple_norm_g, w_ple_gate, w_ple, final_g):
    batch, seq, _ = x.shape
    depth = w_in.shape[0]
    m = batch * seq
    h = x.reshape(m, D_MODEL)
    rows = lambda a: a.reshape(depth, 1, -1)
    g_pair = rows(jnp.concatenate([attn_out_g, attn_out_g], axis=-1))
    p3d = p.reshape(depth, m, PLE_DIM)
    for i in range(depth):
        q0, q1, k, v0, v1, sga, c, sgc = _in_proj(h, rows(norm_g), w_in, i)
        ya = _attention(q0, q1, k, v0, v1, sga, g_pair, i, batch, seq)
        cv = _depthwise_conv(c, dw_w, rows(dw_b), i, batch, seq)
        h = _out_proj(ya, cv, sgc, h, p3d, rows(conv_ln_g), rows(conv_ln_b), w_pw,
                      rows(conv_out_g), w_out, rows(ple_norm_g), w_ple_gate, w_ple,
                      final_g.reshape(1, -1), i, final=(i == depth - 1))
    return h.reshape(batch, seq, D_MODEL)
```

```python
import functools

import jax
import jax.numpy as jnp
from jax import lax
from jax.experimental import pallas as pl
from jax.experimental.pallas import tpu as pltpu

D_MODEL = 1024
ATTN_DIM = 512
CONV_DIM = 512
HEAD_DIM = 64
CONV_WIDTH = 31
PLE_DIM = 256
D_IN = 4 * ATTN_DIM + 3 * CONV_DIM
EPS = 1e-6

LANES = 128
SUBLANES = 8
K_CHUNK = 256
HEADS_IN_FLIGHT = 1
HALO = 32
ROW_TILE = 1024
OUT_TILE = 1024
CONV_TILE = 512
CONV_ROWS = 128
VMEM_LIMIT = 56 * 1024 * 1024
LOG2E = 1.4426950408889634
EXP_UNDERFLOW = -104.0

F32 = jnp.float32
BF16 = jnp.bfloat16


def _sigmoid(x):
    return 1.0 / (1.0 + jnp.exp(-x))


def _cast_once(pairs, grid_rank=1):
    first = pl.program_id(0) == 0
    for axis in range(1, grid_rank):
        first = jnp.logical_and(first, pl.program_id(axis) == 0)

    @pl.when(first)
    def _():
        for src_ref, dst_ref in pairs:
            for c0 in range(0, src_ref.shape[1], ATTN_DIM):
                dst_ref[:, c0:c0 + ATTN_DIM] = src_ref[:, c0:c0 + ATTN_DIM].astype(BF16)


def _in_proj_kernel(x_ref, g_ref, w32_ref, q0_ref, q1_ref, k_ref, v0_ref, v1_ref, sga_ref, c_ref,
                    sgc_ref, w_ref):
    _cast_once([(w32_ref, w_ref)])
    lane = lax.broadcasted_iota(jnp.int32, (1, ATTN_DIM), 1)
    even_head = (lane // HEAD_DIM) % 2 == 0

    def split_heads(t, even_ref, odd_ref, rows):
        even_ref[rows, :] = jnp.where(even_head, t, 0.0).astype(BF16)
        odd_ref[rows, :] = jnp.where(even_head, 0.0, t).astype(BF16)

    def normed(rows):
        x = x_ref[rows, :]
        ms = jnp.mean(x * x, axis=-1, keepdims=True)
        return (x * lax.rsqrt(ms + EPS) * g_ref[...]).astype(BF16)

    def project(rows, hn):
        def seg(i):
            return jnp.dot(hn, w_ref[:, i * ATTN_DIM:(i + 1) * ATTN_DIM],
                           preferred_element_type=F32)

        split_heads(seg(0) * (HEAD_DIM ** -0.5), q0_ref, q1_ref, rows)
        k_ref[rows, :] = seg(1).astype(BF16)
        split_heads(seg(2), v0_ref, v1_ref, rows)
        ga = seg(3)
        sga_ref[rows, :] = ga * _sigmoid(ga)
        c_ref[rows, :] = seg(4) * _sigmoid(seg(5))
        gc = seg(6)
        sgc_ref[rows, :] = gc * _sigmoid(gc)

    n_rows = x_ref.shape[0]
    halves = [slice(0, n_rows // 2), slice(n_rows // 2, n_rows)]
    hns = [normed(rows) for rows in halves]
    for rows, hn in zip(halves, hns):
        project(rows, hn)


def _layer_spec(layer, shape):
    zeros = (0,) * len(shape)
    return pl.BlockSpec((None,) + tuple(shape), lambda *_: (layer,) + zeros,
                        pipeline_mode=pl.Buffered(1))


def _in_proj(h2d, norm_g, w_in, layer):
    m = h2d.shape[0]
    row = lambda i: (i, 0)
    seg_spec = pl.BlockSpec((ROW_TILE, ATTN_DIM), row)
    return pl.pallas_call(
        _in_proj_kernel,
        grid=(m // ROW_TILE,),
        in_specs=[
            pl.BlockSpec((ROW_TILE, D_MODEL), row),
            _layer_spec(layer, (1, D_MODEL)),
            _layer_spec(layer, (D_MODEL, D_IN)),
        ],
        out_specs=[seg_spec] * 8,
        out_shape=[
            jax.ShapeDtypeStruct((m, ATTN_DIM), BF16),
            jax.ShapeDtypeStruct((m, ATTN_DIM), BF16),
            jax.ShapeDtypeStruct((m, ATTN_DIM), BF16),
            jax.ShapeDtypeStruct((m, ATTN_DIM), BF16),
            jax.ShapeDtypeStruct((m, ATTN_DIM), BF16),
            jax.ShapeDtypeStruct((m, ATTN_DIM), F32),
            jax.ShapeDtypeStruct((m, CONV_DIM), F32),
            jax.ShapeDtypeStruct((m, CONV_DIM), F32),
        ],
        scratch_shapes=[pltpu.VMEM((D_MODEL, D_IN), BF16)],
        compiler_params=pltpu.CompilerParams(
            dimension_semantics=("arbitrary",), vmem_limit_bytes=VMEM_LIMIT),
        name="in_proj",
    )(h2d, norm_g, w_in)


def _attn_kernel(q0_ref, q1_ref, k_ref, v0_ref, v1_ref, sg_ref, g_ref, w_ref, o_ref,
                 acc_ref, carry_ref, live_ref):
    seq = k_ref.shape[0]
    n_chunks = seq // K_CHUNK
    n_sub = K_CHUNK // LANES
    row = lax.broadcasted_iota(jnp.int32, (K_CHUNK, K_CHUNK), 0)
    col = lax.broadcasted_iota(jnp.int32, (K_CHUNK, K_CHUNK), 1)
    causal = col < row
    q_refs = (q0_ref, q1_ref)

    def on_rows(m, has_diag, diag_fn, rest_fn):
        n_diag = K_CHUNK if has_diag else 0
        parts = [diag_fn(slice(0, n_diag))] if has_diag else []
        if m > n_diag:
            parts.append(rest_fn(slice(n_diag, m)))
        return parts[0] if len(parts) == 1 else jnp.concatenate(parts, axis=0)

    def head(q0, m, j, has_diag):
        kc = k_ref[pl.ds(j * K_CHUNK, K_CHUNK), :]
        zs, lhs = [], []
        for h in range(2):
            z = lax.dot_general(q_refs[h][pl.ds(q0, m), :], kc, (((1,), (1,)), ((), ())),
                                preferred_element_type=F32)
            sp = jnp.maximum(z, 0.0) + jnp.log(1.0 + jnp.exp2(jnp.abs(z) * -LOG2E))
            sp = on_rows(m, has_diag, lambda d: jnp.where(causal, sp[d], 0.0), lambda b: sp[b])
            hi = sp.astype(BF16)
            lo = (sp - hi.astype(F32)).astype(BF16)
            zs.append(z)
            for s in reversed(range(n_sub)):
                sl = slice(s * LANES, (s + 1) * LANES)
                lhs.append(jnp.concatenate([hi[:, sl], lo[:, sl]], axis=1))
        r = jnp.dot(jnp.concatenate(lhs, axis=0), w_ref[...], preferred_element_type=F32)
        return q0, m, j, has_diag, zs, r

    def tail(q0, m, j, has_diag, zs, r):
        n_diag = K_CHUNK if has_diag else 0
        ps = []
        for h in range(2):
            z = zs[h]
            c = None
            p_sub = [None] * n_sub
            for idx, s in enumerate(reversed(range(n_sub))):
                sl = slice(s * LANES, (s + 1) * LANES)
                rr = r[(h * n_sub + idx) * m:(h * n_sub + idx + 1) * m]
                arg = z[:, sl] + rr[:, :LANES]
                if c is None:
                    c_in = carry_ref[h, pl.ds(q0 + n_diag, m - n_diag), :] if m > n_diag else None
                    arg = on_rows(m, has_diag, lambda d: arg[d], lambda b: arg[b] + c_in)
                    c = on_rows(m, has_diag, lambda d: rr[d, LANES:],
                                lambda b: c_in + rr[b, LANES:])
                else:
                    arg = arg + c
                    c = c + rr[:, LANES:]
                p_sub[s] = jnp.exp(arg)
            carry_ref[h, pl.ds(q0, m), :] = c
            p = jnp.concatenate(p_sub, axis=1)
            p = on_rows(m, has_diag, lambda d: jnp.where(causal, p[d], 0.0), lambda b: p[b])
            ps.append(p.astype(BF16))
        keys = pl.ds(j * K_CHUNK, K_CHUNK)
        vj = jnp.concatenate([v0_ref[keys, :], v1_ref[keys, :]], axis=0)
        pv = jnp.dot(jnp.concatenate(ps, axis=1), vj, preferred_element_type=F32)
        if has_diag:
            acc_ref[pl.ds(q0, n_diag), :] = pv[:n_diag]
        if m > n_diag:
            acc_ref[pl.ds(q0 + n_diag, m - n_diag), :] += pv[n_diag:]

    lane = lax.broadcasted_iota(jnp.int32, (K_CHUNK, LANES), 1)
    first_head = lane < HEAD_DIM

    def emit(q0):
        rows = pl.ds(q0, K_CHUNK)
        o = acc_ref[rows, :]
        o2 = o * o
        ms0 = jnp.sum(jnp.where(first_head, o2, 0.0), axis=1, keepdims=True) * (1.0 / HEAD_DIM)
        ms1 = jnp.sum(jnp.where(first_head, 0.0, o2), axis=1, keepdims=True) * (1.0 / HEAD_DIM)
        inv = jnp.where(first_head, lax.rsqrt(ms0 + EPS), lax.rsqrt(ms1 + EPS))
        y = o * inv * g_ref[...] * sg_ref[rows, :]
        o_ref[rows, :] = y.astype(BF16)

    def alive(q0):
        return jnp.max(carry_ref[:, pl.ds(q0, K_CHUNK), :]) > EXP_UNDERFLOW

    def finish(issued):
        tail(*issued)
        j = issued[2]
        if j + 1 < n_chunks:
            emit((j + 1) * K_CHUNK)
        if j >= 1 and j + 1 < n_chunks:
            live_ref[j + 1] = alive((j + 1) * K_CHUNK).astype(jnp.int32)

    pending = []
    for j in reversed(range(n_chunks)):
        pending.append(head(j * K_CHUNK, min(2 * K_CHUNK, seq - j * K_CHUNK), j, True))
        if len(pending) > HEADS_IN_FLIGHT:
            finish(pending.pop(0))
    for issued in pending:
        finish(issued)
    emit(0)

    def finish_block(i, _):
        q0 = pl.multiple_of(i * K_CHUNK, K_CHUNK)

        def more(state):
            j, live = state
            return jnp.logical_and(j >= 0, live)

        def step(state):
            j, _ = state
            tail(*head(q0, K_CHUNK, j, False))
            return j - 1, alive(q0)

        ran = live_ref[i] > 0
        lax.while_loop(more, step, (i - 2, ran))

        @pl.when(ran)
        def _():
            emit(q0)

        return 0

    lax.fori_loop(2, n_chunks, finish_block, 0)


def _cumsum_weights():
    j = lax.broadcasted_iota(jnp.int32, (LANES, LANES), 0)
    s = lax.broadcasted_iota(jnp.int32, (LANES, LANES), 1)
    neg_u = jnp.where(j >= s, -1.0, 0.0).astype(BF16)
    half = jnp.concatenate([neg_u, jnp.full((LANES, LANES), -1.0, BF16)], axis=1)
    return jnp.concatenate([half, half], axis=0)


def _attention(q0, q1, k, v0, v1, sg, g_pair, layer, batch, seq):
    n_pairs = ATTN_DIM // LANES
    seq_spec = pl.BlockSpec((seq, LANES), lambda b, p: (b, p))
    const = lambda b, p: (0, 0)
    return pl.pallas_call(
        _attn_kernel,
        grid=(batch, n_pairs),
        in_specs=[
            seq_spec, seq_spec, seq_spec, seq_spec, seq_spec, seq_spec,
            _layer_spec(layer, (1, LANES)),
            pl.BlockSpec((2 * LANES, 2 * LANES), const),
        ],
        out_specs=seq_spec,
        out_shape=jax.ShapeDtypeStruct((batch * seq, ATTN_DIM), BF16),
        scratch_shapes=[
            pltpu.VMEM((seq, LANES), F32),
            pltpu.VMEM((2, seq, LANES), F32),
            pltpu.SMEM((seq // K_CHUNK,), jnp.int32),
        ],
        compiler_params=pltpu.CompilerParams(
            dimension_semantics=("arbitrary", "arbitrary"), vmem_limit_bytes=VMEM_LIMIT),
        name="sb_attention",
    )(q0, q1, k, v0, v1, sg, g_pair, _cumsum_weights())


def _conv_kernel(c_ref, halo_ref, dww_ref, dwb_ref, o_ref, xw_ref):
    halo = halo_ref[...]
    xw_ref[0:HALO, :] = jnp.where(pl.program_id(1) == 0, jnp.zeros_like(halo), halo)
    xw_ref[HALO:, :] = c_ref[...]

    base = HALO - (CONV_WIDTH - 1)
    span = HALO + CONV_ROWS
    for r0 in range(0, CONV_TILE, CONV_ROWS):
        for l0 in range(0, CONV_DIM, LANES):
            acc = jnp.zeros((CONV_ROWS, LANES), F32) + dwb_ref[:, l0:l0 + LANES]
            xs = xw_ref[r0:r0 + span, l0:l0 + LANES]
            for phase in range(SUBLANES):
                taps = [w for w in range(CONV_WIDTH) if (base + w) % SUBLANES == phase]
                if not taps:
                    continue
                y = xs if phase == 0 else pltpu.roll(xs, span - phase, axis=0)
                for w in taps:
                    a = base + w - phase
                    acc = acc + y[a:a + CONV_ROWS, :] * dww_ref[w:w + 1, l0:l0 + LANES]
            o_ref[r0:r0 + CONV_ROWS, l0:l0 + LANES] = acc


def _depthwise_conv(c, dw_w, dw_b, layer, batch, seq):
    tiles = seq // CONV_TILE
    halo_per_tile = CONV_TILE // HALO
    tile_spec = pl.BlockSpec((CONV_TILE, CONV_DIM), lambda b, i: (b * tiles + i, 0))
    halo_spec = pl.BlockSpec(
        (HALO, CONV_DIM),
        lambda b, i: (jnp.maximum((b * tiles + i) * halo_per_tile - 1, 0), 0))
    return pl.pallas_call(
        _conv_kernel,
        grid=(batch, tiles),
        in_specs=[
            tile_spec, halo_spec,
            _layer_spec(layer, (CONV_WIDTH, CONV_DIM)),
            _layer_spec(layer, (1, CONV_DIM)),
        ],
        out_specs=tile_spec,
        out_shape=jax.ShapeDtypeStruct((batch * seq, CONV_DIM), F32),
        scratch_shapes=[pltpu.VMEM((CONV_TILE + HALO, CONV_DIM), F32)],
        compiler_params=pltpu.CompilerParams(
            dimension_semantics=("arbitrary", "arbitrary"), vmem_limit_bytes=VMEM_LIMIT),
        name="depthwise_conv",
    )(c, c, dw_w, dw_b)


def _out_proj_kernel(ya_ref, cv_ref, sgc_ref, h_ref, p_ref, lng_ref, lnb_ref, wpw32_ref, og_ref,
                     wo32_ref, ng_ref, wg32_ref, wp32_ref, fg_ref,
                     o_ref, wpw_ref, wo_ref, wg_ref, wp_ref, *, final):
    _cast_once([(wpw32_ref, wpw_ref), (wo32_ref, wo_ref), (wg32_ref, wg_ref), (wp32_ref, wp_ref)])

    def conv_tail(rows):
        c = cv_ref[rows, :]
        mu = jnp.mean(c, axis=-1, keepdims=True)
        cc = c - mu
        var = jnp.mean(cc * cc, axis=-1, keepdims=True)
        y = cc * lax.rsqrt(var + EPS) * lng_ref[...] + lnb_ref[...]
        y = y * _sigmoid(y)
        z = jnp.dot(y.astype(BF16), wpw_ref[...], preferred_element_type=F32)
        ms = jnp.mean(z * z, axis=-1, keepdims=True)
        return (z * lax.rsqrt(ms + EPS) * og_ref[...] * sgc_ref[rows, :]).astype(BF16)

    def mix(rows):
        h = h_ref[rows, :]
        h = h + jnp.dot(ya_ref[rows, :], wo_ref[0:ATTN_DIM, :], preferred_element_type=F32)
        h = h + jnp.dot(conv_tail(rows), wo_ref[ATTN_DIM:, :], preferred_element_type=F32)
        ms = jnp.mean(h * h, axis=-1, keepdims=True)
        return h, (h * lax.rsqrt(ms + EPS) * ng_ref[...]).astype(BF16)

    def embed(rows, h, hn):
        gate = _sigmoid(jnp.dot(hn, wg_ref[...], preferred_element_type=F32))
        e = jnp.dot(p_ref[rows, :].astype(BF16), wp_ref[...], preferred_element_type=F32)
        h = h + e * gate
        if final:
            ms = jnp.mean(h * h, axis=-1, keepdims=True)
            h = h * lax.rsqrt(ms + EPS) * fg_ref[...]
        o_ref[rows, :] = h

    n_rows = h_ref.shape[0]
    halves = [slice(0, n_rows // 2), slice(n_rows // 2, n_rows)]
    mixed = [mix(rows) for rows in halves]
    for rows, (h, hn) in zip(halves, mixed):
        embed(rows, h, hn)


def _out_proj(ya, cv, sgc, h2d, p3d, ln_g, ln_b, w_pw, conv_out_g, w_out, ple_norm_g, w_ple_gate,
              w_ple, final_g, layer, final):
    m = h2d.shape[0]
    row = lambda i: (i, 0)
    conv_vec = _layer_spec(layer, (1, CONV_DIM))
    return pl.pallas_call(
        functools.partial(_out_proj_kernel, final=final),
        grid=(m // OUT_TILE,),
        in_specs=[
            pl.BlockSpec((OUT_TILE, ATTN_DIM), row),
            pl.BlockSpec((OUT_TILE, CONV_DIM), row),
            pl.BlockSpec((OUT_TILE, CONV_DIM), row),
            pl.BlockSpec((OUT_TILE, D_MODEL), row),
            pl.BlockSpec((None, OUT_TILE, PLE_DIM), lambda i: (layer, i, 0)),
            conv_vec, conv_vec,
            _layer_spec(layer, (CONV_DIM, CONV_DIM)),
            conv_vec,
            _layer_spec(layer, (D_MODEL, D_MODEL)),
            _layer_spec(layer, (1, D_MODEL)),
            _layer_spec(layer, (D_MODEL, D_MODEL)),
            _layer_spec(layer, (PLE_DIM, D_MODEL)),
            pl.BlockSpec((1, D_MODEL), lambda i: (0, 0)),
        ],
        out_specs=pl.BlockSpec((OUT_TILE, D_MODEL), row),
        out_shape=jax.ShapeDtypeStruct((m, D_MODEL), F32),
        scratch_shapes=[
            pltpu.VMEM((CONV_DIM, CONV_DIM), BF16),
            pltpu.VMEM((D_MODEL, D_MODEL), BF16),
            pltpu.VMEM((D_MODEL, D_MODEL), BF16),
            pltpu.VMEM((PLE_DIM, D_MODEL), BF16),
        ],
        compiler_params=pltpu.CompilerParams(
            dimension_semantics=("arbitrary",), vmem_limit_bytes=VMEM_LIMIT),
        name="out_proj_final" if final else "out_proj",
    )(ya, cv, sgc, h2d, p3d, ln_g, ln_b, w_pw, conv_out_g, w_out, ple_norm_g, w_ple_gate, w_ple,
      final_g)


def kernel(x, p, norm_g, w_in, attn_out_g, dw_w, dw_b, conv_ln_g, conv_ln_b, w_pw, conv_out_g,
           w_out, ple_norm_g, w_ple_gate, w_ple, final_g):
    batch, seq, _ = x.shape
    depth = w_in.shape[0]
    m = batch * seq
    h = x.reshape(m, D_MODEL)
    rows = lambda a: a.reshape(depth, 1, -1)
    g_pair = rows(jnp.concatenate([attn_out_g, attn_out_g], axis=-1))
    p3d = p.reshape(depth, m, PLE_DIM)
    for i in range(depth):
        q0, q1, k, v0, v1, sga, c, sgc = _in_proj(h, rows(norm_g), w_in, i)
        ya = _attention(q0, q1, k, v0, v1, sga, g_pair, i, batch, seq)
        cv = _depthwise_conv(c, dw_w, rows(dw_b), i, batch, seq)
        h = _out_proj(ya, cv, sgc, h, p3d, rows(conv_ln_g), rows(conv_ln_b), w_pw,
                      rows(conv_out_g), w_out, rows(ple_norm_g), w_ple_gate, w_ple,
                      final_g.reshape(1, -1), i, final=(i == depth - 1))
    return h.reshape(batch, seq, D_MODEL)
```
